```python
import math
import jax
import jax.numpy as jnp
from jax import lax
import numpy as np

D_MODEL = 4096
BATCH = 4
SEQ = 2048
DEPTH = 2
DEC_BATCH = 8
DEC_SEQ = 8
PAST_LEN = 16384
PAGE_SIZE = 128

HEAD_DIM = 128
GDN_HEADS = 12
GDN_DK = HEAD_DIM
GDN_DV = HEAD_DIM
GDN_CONV = 4
GDN_CHUNK = 64
GDN_QK_W = GDN_HEADS * GDN_DK
GDN_V_W = GDN_HEADS * GDN_DV
GDN_CONV_W = 2 * GDN_QK_W + GDN_V_W
MOBA_HEADS = 10
MOBA_BLOCK = 256
MOBA_TOPK = 3
MOBA_QCHUNK = 16
MOBA_W = MOBA_HEADS * HEAD_DIM
DIFF_HEADS = 5
DIFF_QBLOCK = 128
DIFF_QK_W = 2 * DIFF_HEADS * HEAD_DIM
DIFF_V_W = DIFF_HEADS * 2 * HEAD_DIM
N_BRANCH = 3
MIX_W = GDN_V_W + MOBA_W + DIFF_V_W
IN_SIZES = (GDN_CONV_W, GDN_HEADS, GDN_HEADS, GDN_V_W, MOBA_W, MOBA_W, MOBA_W, DIFF_QK_W, DIFF_QK_W, DIFF_V_W, D_MODEL, D_MODEL, D_MODEL)
IN_W = sum(IN_SIZES)
N_EXPERTS = 32
TOP_K = 4
D_EXPERT = D_MODEL // 2
SWIGLU_LIMIT = 7.0
SWIGLU_ALPHA = 1.702
MOE_BLOCK = 128
N_MOD = 6
RMS_EPS = 1e-6
L2_EPS = 1e-6

kernel_name = 'hybrid_gdn_moba_diffattn_moe_adaln_step'


def split_cols(z, sizes):
    return jnp.split(z, np.cumsum(sizes)[:-1].tolist(), axis=-1)


def rms_norm(x, w):
    xf = x.astype(jnp.float32)
    y = xf * lax.rsqrt(jnp.mean(xf * xf, axis=-1, keepdims=True) + RMS_EPS)
    return (y * w.astype(jnp.float32)).astype(x.dtype)


def l2_normalize(x):
    xf = x.astype(jnp.float32)
    return xf * lax.rsqrt(jnp.sum(xf * xf, axis=-1, keepdims=True) + L2_EPS)


def causal_conv(x, buf, w):
    xp = jnp.concatenate([buf.astype(x.dtype), x], axis=1)
    y = lax.conv_general_dilated(xp, w[:, None, :].astype(x.dtype), window_strides=(1,), padding='VALID',
                                 dimension_numbers=('NWC', 'WIO', 'NWC'), feature_group_count=x.shape[-1])
    return y, xp[:, -(GDN_CONV - 1):]


def gdn_chunk_step(state, inp):
    q, k, v, g, beta = inp
    C = q.shape[2]
    G = jnp.cumsum(g, axis=-1)
    incl = jnp.tril(jnp.ones((C, C), dtype=bool))
    strict = jnp.tril(jnp.ones((C, C), dtype=bool), -1)
    decay = jnp.exp(jnp.where(incl, G[..., :, None] - G[..., None, :], -jnp.inf))
    a_mat = jnp.where(strict, beta[..., :, None] * jnp.einsum('bhtd,bhjd->bhtj', k, k) * decay, 0.0)
    gam = jnp.exp(G)
    rhs = jnp.concatenate([beta[..., None] * v, (beta * gam)[..., None] * k], axis=-1)
    sol = lax.linalg.triangular_solve(jnp.eye(C, dtype=jnp.float32) + a_mat, rhs,
                                      left_side=True, lower=True, unit_diagonal=True)
    dv = v.shape[-1]
    u = sol[..., :dv] - jnp.einsum('bhck,bhkv->bhcv', sol[..., dv:], state)
    qk = jnp.einsum('bhtd,bhjd->bhtj', q, k) * decay
    o = jnp.einsum('bht,bhtk,bhkv->bhtv', gam, q, state) + jnp.einsum('bhtj,bhjv->bhtv', qk, u)
    g_last = G[..., -1:]
    new_state = jnp.exp(g_last)[..., None] * state + jnp.einsum('bhjk,bhjv->bhkv', k * jnp.exp(g_last - G)[..., None], u)
    return new_state, o


def gated_delta_rule(q, k, v, g, beta, state):
    B, L, H, _ = q.shape
    C = min(GDN_CHUNK, L)
    n = -(-L // C)
    pad = n * C - L

    def to_chunks(t):
        t = t.astype(jnp.float32)
        t = jnp.pad(t, ((0, 0), (0, pad)) + ((0, 0),) * (t.ndim - 2))
        t = t.reshape((B, n, C) + t.shape[2:])
        return jnp.moveaxis(jnp.moveaxis(t, 3, 2), 1, 0)

    xs = tuple(to_chunks(t) for t in (q, k, v, g, beta))
    final, o = lax.scan(gdn_chunk_step, state.astype(jnp.float32), xs)
    o = jnp.transpose(o, (1, 0, 3, 2, 4)).reshape(B, n * C, H, -1)[:, :L]
    return o, final


def moba_attention(q, k, v, past_len):
    B, Lq, H, D = q.shape
    L = k.shape[1]
    nblk = -(-L // MOBA_BLOCK)
    widths = ((0, 0), (0, nblk * MOBA_BLOCK - L), (0, 0), (0, 0))
    kb = jnp.pad(k, widths).reshape(B, nblk, MOBA_BLOCK, H, D)
    vb = jnp.pad(v, widths).reshape(B, nblk, MOBA_BLOCK, H, D)
    k_mean = jnp.mean(kb.astype(jnp.float32), axis=2)
    q_pos = past_len + jnp.arange(Lq)
    q_blk = q_pos // MOBA_BLOCK
    score = jnp.einsum('bqhd,bnhd->bhqn', q.astype(jnp.float32), k_mean)
    score = jnp.where(jnp.arange(nblk)[None, :] < q_blk[:, None], score, -jnp.inf)
    n_sel = min(MOBA_TOPK, nblk)
    _, sel = lax.top_k(score, n_sel)
    blk = jnp.concatenate([sel, jnp.broadcast_to(q_blk[:, None], (B, H, Lq, 1))], axis=-1)
    valid = jnp.concatenate([jnp.arange(n_sel)[None, :] < q_blk[:, None], jnp.ones((Lq, 1), bool)], axis=-1)
    ns = n_sel + 1
    qc = MOBA_QCHUNK if Lq % MOBA_QCHUNK == 0 else Lq
    nq = Lq // qc
    bi = jnp.arange(B)[:, None, None, None]
    hi = jnp.arange(H)[None, :, None, None]
    offs = jnp.arange(MOBA_BLOCK)
    scale = D ** -0.5

    def attend_chunk(args):
        q_ch, blk_ch, valid_ch, pos_ch = args
        k_g = kb[bi, blk_ch, :, hi]
        v_g = vb[bi, blk_ch, :, hi]
        s = jnp.einsum('bqhd,bhqnkd->bhqnk', q_ch, k_g).astype(jnp.float32) * scale
        k_pos = blk_ch[..., None] * MOBA_BLOCK + offs
        mask = valid_ch[:, :, None] & (k_pos <= pos_ch[:, None, None])
        s = jnp.where(mask, s, -jnp.inf)
        p = jax.nn.softmax(s.reshape(s.shape[:3] + (-1,)), axis=-1).reshape(s.shape)
        return jnp.einsum('bhqnk,bhqnkd->bqhd', p.astype(v.dtype), v_g)

    q_c = jnp.moveaxis(q.reshape(B, nq, qc, H, D), 1, 0)
    blk_c = jnp.moveaxis(blk.reshape(B, H, nq, qc, ns), 2, 0)
    o = lax.map(attend_chunk, (q_c, blk_c, valid.reshape(nq, qc, ns), q_pos.reshape(nq, qc)))
    return jnp.moveaxis(o, 0, 1).reshape(B, Lq, H, D)


def diff_attention(q, k, v, past_len, lam):
    B, Lq, H, _, D = q.shape
    L = k.shape[1]
    qb = DIFF_QBLOCK if Lq % DIFF_QBLOCK == 0 else Lq
    nq = Lq // qb
    k_pos = jnp.arange(L)
    scale = D ** -0.5

    def attend_block(args):
        q_blk, pos = args
        s = jnp.einsum('bqhid,bkhid->bhiqk', q_blk, k).astype(jnp.float32) * scale
        s = jnp.where(k_pos[None, :] <= pos[:, None], s, -jnp.inf)
        p = jax.nn.softmax(s, axis=-1)
        w = p[:, :, 0] - lam * p[:, :, 1]
        return jnp.einsum('bhqk,bkhe->bqhe', w.astype(v.dtype), v)

    q_c = jnp.moveaxis(q.reshape(B, nq, qb, H, 2, D), 1, 0)
    pos_c = (past_len + jnp.arange(Lq)).reshape(nq, qb)
    o = lax.map(attend_block, (q_c, pos_c))
    return jnp.moveaxis(o, 0, 1).reshape(B, Lq, H, 2 * D)


def moe_ffn(h, l, W):
    T, D = h.shape
    logits = (h @ W['w_router'][l] + W['b_router'][l]).astype(jnp.float32)
    top_val, top_idx = lax.top_k(logits, TOP_K)
    gates = jax.nn.softmax(top_val, axis=-1)
    A = T * TOP_K
    bk = max(1, min(MOE_BLOCK, A // N_EXPERTS))
    nb = (A + N_EXPERTS * (bk - 1)) // bk
    flat_e = top_idx.reshape(A)
    flat_t = jnp.repeat(jnp.arange(T, dtype=jnp.int32), TOP_K)
    flat_g = gates.reshape(A)
    order = jnp.argsort(flat_e)
    se = flat_e[order]
    counts = jnp.zeros((N_EXPERTS,), jnp.int32).at[flat_e].add(1)
    padded = (counts + bk - 1) // bk * bk
    grp_start = jnp.cumsum(counts) - counts
    pad_end = jnp.cumsum(padded)
    pad_start = pad_end - padded
    dest = pad_start[se] + jnp.arange(A) - grp_start[se]
    P = nb * bk
    slot_tok = jnp.full((P,), T, jnp.int32).at[dest].set(flat_t[order])
    slot_gate = jnp.zeros((P,), jnp.float32).at[dest].set(flat_g[order])
    blk_exp = jnp.minimum(jnp.searchsorted(pad_end, jnp.arange(nb) * bk, side='right'), N_EXPERTS - 1)
    h_pad = jnp.concatenate([h, jnp.zeros((1, D), h.dtype)], axis=0)
    xb = h_pad[slot_tok].reshape(nb, bk, D)
    w_gu, b_gu, w_dn, b_dn = W['w_gu'], W['b_gu'], W['w_dn'], W['b_dn']

    def expert_block(args):
        x_blk, e = args
        gu = x_blk @ w_gu[l, e] + b_gu[l, e]
        gate = jnp.minimum(gu[:, :D_EXPERT], SWIGLU_LIMIT)
        up = jnp.clip(gu[:, D_EXPERT:], -SWIGLU_LIMIT, SWIGLU_LIMIT)
        act = (up + 1.0) * gate * jax.nn.sigmoid(SWIGLU_ALPHA * gate)
        return act @ w_dn[l, e] + b_dn[l, e]

    yb = lax.map(expert_block, (xb, blk_exp)).reshape(P, D)
    y = jnp.zeros((T + 1, D), yb.dtype).at[slot_tok].add(yb * slot_gate[:, None].astype(yb.dtype))
    return y[:T].astype(h.dtype)


def gather_pages(cache, l, page_table):
    g = cache[l, page_table]
    return g.reshape((page_table.shape[0], page_table.shape[1] * PAGE_SIZE) + cache.shape[3:])


def mixer_block(h, l, past_len, gdn_state, conv_buf, past_kv, W):
    B, L, _ = h.shape
    z = h @ W['w_in'][l]
    (qkv_a, dec_a, beta_a, gate_a, q_b, k_b, v_b, q_c, k_c, v_c,
     br_a, br_b, br_c) = split_cols(z, IN_SIZES)
    qkv_a, conv_new = causal_conv(qkv_a, conv_buf, W['conv_w'][l])
    qa, ka, va = split_cols(jax.nn.silu(qkv_a), (GDN_QK_W, GDN_QK_W, GDN_V_W))
    qa = l2_normalize(qa.reshape(B, L, GDN_HEADS, GDN_DK)) * GDN_DK ** -0.5
    ka = l2_normalize(ka.reshape(B, L, GDN_HEADS, GDN_DK))
    va = va.reshape(B, L, GDN_HEADS, GDN_DV)
    g = -jnp.exp(W['a_log'][l].astype(jnp.float32)) * jax.nn.softplus(dec_a.astype(jnp.float32) + W['dt_bias'][l].astype(jnp.float32))
    beta = jax.nn.sigmoid(beta_a.astype(jnp.float32))
    o_a, gdn_new = gated_delta_rule(qa, ka, va, g, beta, gdn_state)
    o_a = rms_norm(o_a, W['gdn_norm_w'][l]) * jax.nn.silu(gate_a.astype(jnp.float32).reshape(B, L, GDN_HEADS, GDN_DV))
    o_a = o_a.reshape(B, L, GDN_V_W).astype(h.dtype)
    qb = q_b.reshape(B, L, MOBA_HEADS, HEAD_DIM)
    kb = k_b.reshape(B, L, MOBA_HEADS, HEAD_DIM)
    vb = v_b.reshape(B, L, MOBA_HEADS, HEAD_DIM)
    qc = q_c.reshape(B, L, DIFF_HEADS, 2, HEAD_DIM)
    kc = k_c.reshape(B, L, DIFF_HEADS, 2, HEAD_DIM)
    vc = v_c.reshape(B, L, DIFF_HEADS, 2 * HEAD_DIM)
    if past_kv is None:
        kb_all, vb_all, kc_all, vc_all = kb, vb, kc, vc
    else:
        pkb, pvb, pkc, pvc = past_kv
        kb_all = jnp.concatenate([pkb.astype(kb.dtype), kb], axis=1)
        vb_all = jnp.concatenate([pvb.astype(vb.dtype), vb], axis=1)
        kc_all = jnp.concatenate([pkc.reshape(B, past_len, DIFF_HEADS, 2, HEAD_DIM).astype(kc.dtype), kc], axis=1)
        vc_all = jnp.concatenate([pvc.astype(vc.dtype), vc], axis=1)
    o_b = moba_attention(qb, kb_all, vb_all, past_len).reshape(B, L, MOBA_W)
    lam_init = 0.8 - 0.6 * math.exp(-0.3 * l)
    lq1, lk1, lq2, lk2 = W['diff_lambda'][l].astype(jnp.float32)
    lam = jnp.exp(jnp.sum(lq1 * lk1)) - jnp.exp(jnp.sum(lq2 * lk2)) + lam_init
    o_c = diff_attention(qc, kc_all, vc_all, past_len, lam)
    o_c = (rms_norm(o_c, W['diff_norm_w'][l]) * (1.0 - lam_init)).reshape(B, L, DIFF_V_W)
    wb = W['w_branch'][l]
    merged = (jax.nn.sigmoid(br_a) * (o_a @ wb[:GDN_V_W])
              + jax.nn.sigmoid(br_b) * (o_b @ wb[GDN_V_W:GDN_V_W + MOBA_W])
              + jax.nn.sigmoid(br_c) * (o_c @ wb[GDN_V_W + MOBA_W:]))
    out = merged @ W['w_out'][l]
    rows = (kb, vb, kc.reshape(B, L, 2 * DIFF_HEADS, HEAD_DIM), vc)
    return out, gdn_new, conv_new, rows


def trunk(x, c, past, W):
    B, L, D = x.shape
    new_gdn, new_conv, new_mk, new_mv, new_dk, new_dv = [], [], [], [], [], []
    for l in range(DEPTH):
        mod = (jax.nn.silu(c.astype(jnp.float32)) @ W['w_mod'][l].astype(jnp.float32)
               + W['b_mod'][l].astype(jnp.float32)).astype(x.dtype)
        sh1, sc1, g1, sh2, sc2, g2 = jnp.split(mod[:, None, :], N_MOD, axis=-1)
        if past is None:
            past_len = 0
            gdn_state = jnp.zeros((B, GDN_HEADS, GDN_DK, GDN_DV), jnp.float32)
            conv_buf = jnp.zeros((B, GDN_CONV - 1, GDN_CONV_W), x.dtype)
            past_kv = None
        else:
            pt = past['page_table']
            past_len = pt.shape[1] * PAGE_SIZE
            gdn_state = past['state_gdn'][l]
            conv_buf = past['state_conv'][l]
            past_kv = (gather_pages(past['cache_moba_k'], l, pt), gather_pages(past['cache_moba_v'], l, pt),
                       gather_pages(past['cache_diff_k'], l, pt), gather_pages(past['cache_diff_v'], l, pt))
        h = rms_norm(x, W['norm_mix'][l]) * (1.0 + sc1) + sh1
        mix, gdn_s, conv_s, rows = mixer_block(h, l, past_len, gdn_state, conv_buf, past_kv, W)
        x = x + g1 * mix
        h = rms_norm(x, W['norm_ffn'][l]) * (1.0 + sc2) + sh2
        x = x + g2 * moe_ffn(h.reshape(B * L, D), l, W).reshape(B, L, D)
        new_gdn.append(gdn_s)
        new_conv.append(conv_s)
        new_mk.append(rows[0])
        new_mv.append(rows[1])
        new_dk.append(rows[2])
        new_dv.append(rows[3])
    y = rms_norm(x, W['norm_out'])
    return (y, jnp.stack(new_gdn), jnp.stack(new_conv), jnp.stack(new_mk), jnp.stack(new_mv),
            jnp.stack(new_dk), jnp.stack(new_dv))


def setup_inputs(seed: int = 0) -> dict:
    key = jax.random.key(seed)
    ks = iter(jax.random.split(key, 40))

    def nrm(shape, scale):
        return jax.random.normal(next(ks), shape, jnp.float32) * scale

    n_pages = PAST_LEN // PAGE_SIZE
    n_used = DEC_BATCH * n_pages
    n_pool = n_used + max(1, n_used // 4)
    x_prompt = nrm((BATCH, SEQ, D_MODEL), 1.0)
    x_sample = nrm((DEC_BATCH, DEC_SEQ, D_MODEL), 1.0)
    cache_moba_k = nrm((DEPTH, n_pool, PAGE_SIZE, MOBA_HEADS, HEAD_DIM), 1.0)
    cache_moba_v = nrm((DEPTH, n_pool, PAGE_SIZE, MOBA_HEADS, HEAD_DIM), 1.0)
    cache_diff_k = nrm((DEPTH, n_pool, PAGE_SIZE, 2 * DIFF_HEADS, HEAD_DIM), 1.0)
    cache_diff_v = nrm((DEPTH, n_pool, PAGE_SIZE, DIFF_HEADS, 2 * HEAD_DIM), 1.0)
    state_gdn = nrm((DEPTH, DEC_BATCH, GDN_HEADS, GDN_DK, GDN_DV), 0.1)
    state_conv = nrm((DEPTH, DEC_BATCH, GDN_CONV - 1, GDN_CONV_W), 1.0)
    page_table = jax.random.permutation(next(ks), n_pool)[:n_used].reshape(DEC_BATCH, n_pages).astype(jnp.int32)
    c_prompt = nrm((BATCH, D_MODEL), 1.0)
    c_sample = nrm((DEC_BATCH, D_MODEL), 1.0)
    w_mod = nrm((DEPTH, D_MODEL, N_MOD * D_MODEL), 0.5 * D_MODEL ** -0.5)
    b_mod = nrm((DEPTH, N_MOD * D_MODEL), 0.01)
    norm_mix = 1.0 + nrm((DEPTH, D_MODEL), 0.02)
    norm_ffn = 1.0 + nrm((DEPTH, D_MODEL), 0.02)
    w_in = nrm((DEPTH, D_MODEL, IN_W), D_MODEL ** -0.5)
    conv_w = nrm((DEPTH, GDN_CONV, GDN_CONV_W), GDN_CONV ** -0.5)
    a_log = jnp.log(jax.random.uniform(next(ks), (DEPTH, GDN_HEADS), jnp.float32, 1.0, 16.0))
    dt = jnp.exp(jax.random.uniform(next(ks), (DEPTH, GDN_HEADS), jnp.float32, math.log(1e-3), math.log(1e-1)))
    dt_bias = dt + jnp.log(-jnp.expm1(-dt))
    gdn_norm_w = 1.0 + nrm((DEPTH, GDN_DV), 0.02)
    diff_lambda = nrm((DEPTH, 4, HEAD_DIM), 0.1)
    diff_norm_w = 1.0 + nrm((DEPTH, 2 * HEAD_DIM), 0.02)
    w_branch = nrm((DEPTH, MIX_W, D_MODEL), (MIX_W // N_BRANCH) ** -0.5)
    w_out = nrm((DEPTH, D_MODEL, D_MODEL), D_MODEL ** -0.5)
    w_router = nrm((DEPTH, D_MODEL, N_EXPERTS), D_MODEL ** -0.5)
    b_router = nrm((DEPTH, N_EXPERTS), 0.01)
    w_gu = nrm((DEPTH, N_EXPERTS, D_MODEL, 2 * D_EXPERT), D_MODEL ** -0.5)
    b_gu = nrm((DEPTH, N_EXPERTS, 2 * D_EXPERT), 0.01)
    w_dn = nrm((DEPTH, N_EXPERTS, D_EXPERT, D_MODEL), D_EXPERT ** -0.5)
    b_dn = nrm((DEPTH, N_EXPERTS, D_MODEL), 0.01)
    norm_out = 1.0 + nrm((D_MODEL,), 0.02)
    return {'x_prompt': x_prompt, 'x_sample': x_sample,
            'cache_moba_k': cache_moba_k, 'cache_moba_v': cache_moba_v,
            'cache_diff_k': cache_diff_k, 'cache_diff_v': cache_diff_v,
            'state_gdn': state_gdn, 'state_conv': state_conv, 'page_table': page_table,
            'c_prompt': c_prompt, 'c_sample': c_sample,
            'w_mod': w_mod, 'b_mod': b_mod, 'norm_mix': norm_mix, 'norm_ffn': norm_ffn,
            'w_in': w_in, 'conv_w': conv_w, 'a_log': a_log, 'dt_bias': dt_bias, 'gdn_norm_w': gdn_norm_w,
            'diff_lambda': diff_lambda, 'diff_norm_w': diff_norm_w, 'w_branch': w_branch, 'w_out': w_out,
            'w_router': w_router, 'b_router': b_router, 'w_gu': w_gu, 'b_gu': b_gu, 'w_dn': w_dn, 'b_dn': b_dn,
            'norm_out': norm_out}


def reference(x_prompt, x_sample, cache_moba_k, cache_moba_v, cache_diff_k, cache_diff_v, state_gdn, state_conv,
              page_table, c_prompt, c_sample, w_mod, b_mod, norm_mix, norm_ffn, w_in, conv_w, a_log, dt_bias,
              gdn_norm_w, diff_lambda, diff_norm_w, w_branch, w_out, w_router, b_router, w_gu, b_gu, w_dn, b_dn,
              norm_out):
    W = dict(w_mod=w_mod, b_mod=b_mod, norm_mix=norm_mix, norm_ffn=norm_ffn, w_in=w_in, conv_w=conv_w,
             a_log=a_log, dt_bias=dt_bias, gdn_norm_w=gdn_norm_w, diff_lambda=diff_lambda,
             diff_norm_w=diff_norm_w, w_branch=w_branch, w_out=w_out, w_router=w_router, b_router=b_router,
             w_gu=w_gu, b_gu=b_gu, w_dn=w_dn, b_dn=b_dn, norm_out=norm_out)
    y_prompt, gdn_p, conv_p, mk_p, mv_p, dk_p, dv_p = trunk(x_prompt, c_prompt, None, W)
    past = dict(page_table=page_table, state_gdn=state_gdn, state_conv=state_conv,
                cache_moba_k=cache_moba_k, cache_moba_v=cache_moba_v,
                cache_diff_k=cache_diff_k, cache_diff_v=cache_diff_v)
    y_sample, gdn_s, conv_s, mk_s, mv_s, dk_s, dv_s = trunk(x_sample, c_sample, past, W)
    return (y_prompt, y_sample, gdn_p, conv_p, mk_p, mv_p, dk_p, dv_p, gdn_s, conv_s, mk_s, mv_s, dk_s, dv_s)
```

```python
import functools
import math

import jax
import jax.numpy as jnp
from jax import lax
from jax.experimental import pallas as pl
from jax.experimental.pallas import tpu as pltpu

f32 = jnp.float32
bf16 = jnp.bfloat16

HEAD_DIM = 128
GDN_HEADS = 12
GDN_CONV = 4
GDN_CHUNK = 64
GDN_W = GDN_HEADS * HEAD_DIM
GDN_CONV_W = 3 * GDN_W
MOBA_HEADS = 10
MOBA_BLOCK = 256
MOBA_TOPK = 3
MOBA_W = MOBA_HEADS * HEAD_DIM
DIFF_HEADS = 5
DIFF_W = 2 * DIFF_HEADS * HEAD_DIM
N_EXPERTS = 32
TOP_K = 4
SWIGLU_LIMIT = 7.0
SWIGLU_ALPHA = 1.702
N_MOD = 6
RMS_EPS = 1e-6
L2_EPS = 1e-6
PAGE_SIZE = 128

LANES = 128
GDN_TILE = 256
GDN_HEAD_GROUP = 4
ATT_QTILE = 256
DEC_PAGES_PER_STEP = 4
MOE_BM = 256
VMEM_LIMIT_BYTES = 56 * 1024 * 1024

Z_GATE_A = GDN_CONV_W
Z_QB = Z_GATE_A + GDN_W
Z_KB = Z_QB + MOBA_W
Z_VB = Z_KB + MOBA_W
Z_QC = Z_VB + MOBA_W
Z_KC = Z_QC + DIFF_W
Z_VC = Z_KC + DIFF_W
Z_BR = Z_VC + DIFF_W


def _cparams(*sem):
    return pltpu.CompilerParams(dimension_semantics=sem, vmem_limit_bytes=VMEM_LIMIT_BYTES)


def _dot(a, b):
    return jnp.dot(a, b, preferred_element_type=f32)


def _dot_nt(a, b, precision=None):
    return lax.dot_general(a, b, (((1,), (1,)), ((), ())), preferred_element_type=f32, precision=precision)


def _dot_tn(a, b):
    return lax.dot_general(a, b, (((0,), (0,)), ((), ())), preferred_element_type=f32)


def _dot_hi(a, b):
    return jnp.dot(a, b, preferred_element_type=f32, precision=lax.Precision.HIGHEST)


def _sigmoid(x):
    return 1.0 / (1.0 + jnp.exp(-x))


def _norm_body(x_ref, w_ref, *rest, modulate):
    o_ref = rest[-1]
    x = x_ref[0]
    y = x * lax.rsqrt(jnp.mean(x * x, axis=-1, keepdims=True) + RMS_EPS) * w_ref[...]
    if modulate:
        sc_ref, sh_ref = rest[0], rest[1]
        y = y * (1.0 + sc_ref[0]) + sh_ref[0]
    o_ref[0] = y.astype(o_ref.dtype)


def rms_norm_mod(x, w, sc=None, sh=None, out_dtype=bf16):
    B, L, D = x.shape
    tl = min(L, 512)
    modulate = sc is not None
    in_specs = [pl.BlockSpec((1, tl, D), lambda b, i: (b, i, 0)),
                pl.BlockSpec((1, D), lambda b, i: (0, 0))]
    args = [x, w.reshape(1, D)]
    if modulate:
        in_specs += [pl.BlockSpec((1, 1, D), lambda b, i: (b, 0, 0))] * 2
        args += [sc.reshape(B, 1, D), sh.reshape(B, 1, D)]
    return pl.pallas_call(
        functools.partial(_norm_body, modulate=modulate),
        out_shape=jax.ShapeDtypeStruct((B, L, D), out_dtype),
        grid=(B, L // tl),
        in_specs=in_specs,
        out_specs=pl.BlockSpec((1, tl, D), lambda b, i: (b, i, 0)),
        compiler_params=_cparams("parallel", "parallel"),
        name="rms_norm_mod",
    )(*args)


def _mm_body(a_ref, b_ref, *rest, has_bias):
    o_ref = rest[-1]
    acc = _dot(a_ref[...].astype(bf16), b_ref[...].astype(bf16))
    if has_bias:
        acc = acc + rest[0][...]
    o_ref[...] = acc.astype(o_ref.dtype)


def matmul(a, w, *, layer=None, bias=None, tm, tn, out_dtype=f32, name="matmul"):
    M, K = a.shape
    N = w.shape[-1]
    assert M % tm == 0 and N % tn == 0
    if layer is None:
        w_spec = pl.BlockSpec((K, tn), lambda i, j: (0, j))
    else:
        w_spec = pl.BlockSpec((None, K, tn), lambda i, j: (layer, 0, j))
    in_specs = [pl.BlockSpec((tm, K), lambda i, j: (i, 0)), w_spec]
    args = [a, w]
    if bias is not None:
        in_specs.append(pl.BlockSpec((1, tn), lambda i, j: (0, j)))
        args.append(bias.reshape(1, N))
    return pl.pallas_call(
        functools.partial(_mm_body, has_bias=bias is not None),
        out_shape=jax.ShapeDtypeStruct((M, N), out_dtype),
        grid=(M // tm, N // tn),
        in_specs=in_specs,
        out_specs=pl.BlockSpec((tm, tn), lambda i, j: (i, j)),
        compiler_params=_cparams("parallel", "parallel"),
        name=name,
    )(*args)


def _softplus(x):
    return jnp.maximum(x, 0.0) + jnp.log1p(jnp.exp(-jnp.abs(x)))


def _gdn_prep_body(x_ref, prev_ref, buf_ref, cw_ref, zs_ref, al_ref, dtb_ref,
                   q_ref, k_ref, v_ref, g_ref, beta_ref, *, tl, valid_len):
    i = pl.program_id(1)
    row = lax.broadcasted_iota(jnp.int32, (tl, 1), 0)
    valid = (i * tl + row) < valid_len
    row8 = lax.broadcasted_iota(jnp.int32, (8, LANES), 0)
    first = i == 0
    outs = (q_ref, k_ref, v_ref)
    for part in range(3):
        for h in range(GDN_HEADS):
            c0 = part * GDN_W + h * HEAD_DIM
            cols = slice(c0, c0 + HEAD_DIM)
            x = x_ref[0, :, cols]
            prev = jnp.where(first, buf_ref[0, :, cols], prev_ref[0, :, cols])
            y = x * cw_ref[GDN_CONV - 1:GDN_CONV, cols]
            for s in range(1, GDN_CONV):
                xs = pltpu.roll(x, s, axis=0)
                top = jnp.where(row8 < s, pltpu.roll(prev, s, axis=0), xs[:8])
                xs = jnp.concatenate([top, xs[8:]], axis=0)
                y = y + xs * cw_ref[GDN_CONV - 1 - s:GDN_CONV - s, cols]
            y = y * _sigmoid(y)
            if part < 2:
                y = y * lax.rsqrt(jnp.sum(y * y, axis=-1, keepdims=True) + L2_EPS)
            if part == 0:
                y = y * (HEAD_DIM ** -0.5)
            outs[part][0, :, h * HEAD_DIM:(h + 1) * HEAD_DIM] = jnp.where(valid, y, 0.0)
    zs = zs_ref[0]
    g = -jnp.exp(al_ref[...]) * _softplus(zs[:, :LANES] + dtb_ref[...])
    g = jnp.where(valid, g, 0.0)
    beta_ref[0] = jnp.where(valid, _sigmoid(zs[:, LANES:]), 0.0)
    in_chunk = row % GDN_CHUNK
    s = 1
    while s < GDN_CHUNK:
        g = g + jnp.where(in_chunk >= s, pltpu.roll(g, s, axis=0), 0.0)
        s *= 2
    g_ref[0] = g


def gdn_prep(zqkv, zs, buf8, conv_w, a_log, dt_bias, valid_len):
    B, L, _ = zqkv.shape
    tl = GDN_TILE
    assert L % tl == 0
    nprev = tl // 8
    pad = lambda t: jnp.pad(t.astype(f32), (0, LANES - GDN_HEADS)).reshape(1, LANES)
    row_spec = lambda w: pl.BlockSpec((1, tl, w), lambda b, i: (b, i, 0))
    outs = pl.pallas_call(
        functools.partial(_gdn_prep_body, tl=tl, valid_len=valid_len),
        out_shape=[jax.ShapeDtypeStruct((B, L, GDN_W), f32)] * 3 + [jax.ShapeDtypeStruct((B, L, LANES), f32)] * 2,
        grid=(B, L // tl),
        in_specs=[row_spec(GDN_CONV_W),
                  pl.BlockSpec((1, 8, GDN_CONV_W), lambda b, i: (b, jnp.maximum(i * nprev - 1, 0), 0)),
                  pl.BlockSpec((1, 8, GDN_CONV_W), lambda b, i: (b, 0, 0)),
                  pl.BlockSpec((GDN_CONV, GDN_CONV_W), lambda b, i: (0, 0)),
                  row_spec(2 * LANES),
                  pl.BlockSpec((1, LANES), lambda b, i: (0, 0)),
                  pl.BlockSpec((1, LANES), lambda b, i: (0, 0))],
        out_specs=[row_spec(GDN_W)] * 3 + [row_spec(LANES)] * 2,
        compiler_params=_cparams("parallel", "parallel"),
        name="gdn_prep",
    )(zqkv, zqkv, buf8, conv_w, zs, pad(a_log), pad(dt_bias))
    return outs


def _gdn_intra_body(q_ref, k_ref, v_ref, g_ref, beta_ref,
                    w1_ref, w2_ref, qk_ref, qg_ref, kd_ref):
    tl = GDN_TILE
    h = pl.program_id(1)
    lane = lax.broadcasted_iota(jnp.int32, (tl, LANES), 1)
    gcol = jnp.sum(jnp.where(lane == h, g_ref[0], 0.0), axis=-1, keepdims=True)
    bcol = jnp.sum(jnp.where(lane == h, beta_ref[0], 0.0), axis=-1, keepdims=True)
    r = lax.broadcasted_iota(jnp.int32, (tl, tl), 0)
    c = lax.broadcasted_iota(jnp.int32, (tl, tl), 1)
    same = (r // GDN_CHUNK) == (c // GDN_CHUNK)
    eye = r == c
    grow = jnp.sum(jnp.where(eye, gcol, 0.0), axis=0, keepdims=True)
    decay = jnp.exp(jnp.where(same & (r >= c), gcol - grow, -jnp.inf))
    k = k_ref[0]
    q = q_ref[0]
    kb = k.astype(bf16)
    kk = _dot_nt(kb, kb)
    n = jnp.where(same & (r > c), -(bcol * kk * decay), 0.0)
    t = jnp.where(eye, 1.0, 0.0) + n
    p = n
    sq = 1
    while sq < GDN_CHUNK // 2:
        p = _dot_hi(p, p)
        t = t + _dot_hi(t, p)
        sq *= 2
    gam = jnp.exp(gcol)
    w1_ref[0] = _dot_hi(t, bcol * v_ref[0])
    w2_ref[0] = _dot_hi(t, (bcol * gam) * k)
    qk_ref[0, 0] = _dot_nt(q.astype(bf16), kb) * decay
    qg_ref[0] = q * gam
    is_last = c == (r // GDN_CHUNK) * GDN_CHUNK + (GDN_CHUNK - 1)
    glast = jnp.sum(jnp.where(is_last, grow, 0.0), axis=-1, keepdims=True)
    kd_ref[0] = k * jnp.exp(glast - gcol)


def gdn_intra(q, k, v, g, beta):
    B, L, _ = q.shape
    tl = GDN_TILE
    head_spec = pl.BlockSpec((1, tl, HEAD_DIM), lambda b, h, i: (b, i, h))
    all_spec = pl.BlockSpec((1, tl, LANES), lambda b, h, i: (b, i, 0))
    wide = jax.ShapeDtypeStruct((B, L, GDN_W), f32)
    return pl.pallas_call(
        _gdn_intra_body,
        out_shape=[wide, wide, jax.ShapeDtypeStruct((B, GDN_HEADS, L, tl), f32), wide, wide],
        grid=(B, GDN_HEADS, L // tl),
        in_specs=[head_spec, head_spec, head_spec, all_spec, all_spec],
        out_specs=[head_spec, head_spec,
                   pl.BlockSpec((1, 1, tl, tl), lambda b, h, i: (b, h, i, 0)),
                   head_spec, head_spec],
        compiler_params=_cparams("parallel", "parallel", "parallel"),
        name="gdn_intra",
    )(q, k, v, g, beta)


def _gdn_seq_body(w1_ref, w2_ref, qg_ref, kd_ref, qk_ref, g_ref, gate_ref, s0_ref, nw_ref,
                  o_ref, so_ref, s_scr, u_scr):
    tl = GDN_TILE
    hg = pl.program_id(1)
    ti = pl.program_id(2)

    @pl.when(ti == 0)
    def _():
        s_scr[...] = s0_ref[0]

    u_scr[...] = jnp.zeros_like(u_scr)
    lane = lax.broadcasted_iota(jnp.int32, (1, LANES), 1)
    for ci in range(tl // GDN_CHUNK):
        rows = slice(ci * GDN_CHUNK, (ci + 1) * GDN_CHUNK)
        last = ci * GDN_CHUNK + GDN_CHUNK - 1
        g_last_row = g_ref[0, last:last + 1, :]
        for hh in range(GDN_HEAD_GROUP):
            cols = slice(hh * HEAD_DIM, (hh + 1) * HEAD_DIM)
            s = s_scr[hh]
            sb = s.astype(bf16)
            u = w1_ref[0, rows, cols] - _dot(w2_ref[0, rows, cols].astype(bf16), sb)
            u_scr[hh, rows, :] = u
            o = _dot(qg_ref[0, rows, cols].astype(bf16), sb) + _dot(qk_ref[0, hh, rows, :].astype(bf16),
                                                                  u_scr[hh].astype(bf16))
            g_last = jnp.sum(jnp.where(lane == hg * GDN_HEAD_GROUP + hh, g_last_row, 0.0), axis=-1, keepdims=True)
            s_scr[hh] = jnp.exp(g_last) * s + _dot_tn(kd_ref[0, rows, cols].astype(bf16), u.astype(bf16))
            y = o * lax.rsqrt(jnp.mean(o * o, axis=-1, keepdims=True) + RMS_EPS) * nw_ref[...]
            gate = gate_ref[0, rows, cols]
            o_ref[0, rows, cols] = (y * (gate * _sigmoid(gate))).astype(o_ref.dtype)

    @pl.when(ti == pl.num_programs(2) - 1)
    def _():
        so_ref[0] = s_scr[...]


def gdn_seq(w1, w2, qg, kd, qk, g, gate_arr, gate_col0, state0, norm_w):
    B, L, _ = w1.shape
    tl = GDN_TILE
    gw = GDN_HEAD_GROUP * HEAD_DIM
    assert gate_col0 % gw == 0
    gcb = gate_col0 // gw
    grp = pl.BlockSpec((1, tl, gw), lambda b, hg, i: (b, i, hg))
    return pl.pallas_call(
        _gdn_seq_body,
        out_shape=[jax.ShapeDtypeStruct((B, L, GDN_W), bf16),
                   jax.ShapeDtypeStruct((B, GDN_HEADS, HEAD_DIM, HEAD_DIM), f32)],
        grid=(B, GDN_HEADS // GDN_HEAD_GROUP, L // tl),
        in_specs=[grp, grp, grp, grp,
                  pl.BlockSpec((1, GDN_HEAD_GROUP, tl, tl), lambda b, hg, i: (b, hg, i, 0)),
                  pl.BlockSpec((1, tl, LANES), lambda b, hg, i: (b, i, 0)),
                  pl.BlockSpec((1, tl, gw), lambda b, hg, i: (b, i, gcb + hg)),
                  pl.BlockSpec((1, GDN_HEAD_GROUP, HEAD_DIM, HEAD_DIM), lambda b, hg, i: (b, hg, 0, 0)),
                  pl.BlockSpec((1, HEAD_DIM), lambda b, hg, i: (0, 0))],
        out_specs=[grp,
                   pl.BlockSpec((1, GDN_HEAD_GROUP, HEAD_DIM, HEAD_DIM), lambda b, hg, i: (b, hg, 0, 0))],
        scratch_shapes=[pltpu.VMEM((GDN_HEAD_GROUP, HEAD_DIM, HEAD_DIM), f32),
                        pltpu.VMEM((GDN_HEAD_GROUP, tl, HEAD_DIM), f32)],
        compiler_params=_cparams("parallel", "parallel", "arbitrary"),
        name="gdn_seq",
    )(w1, w2, qg, kd, qk, g, gate_arr, state0, norm_w.reshape(1, HEAD_DIM))


def _topk_mask(score, lane_or_row, n_valid, axis, size):
    sel = jnp.zeros(score.shape, jnp.bool_)
    for r in range(MOBA_TOPK):
        m = jnp.max(score, axis=axis, keepdims=True)
        idx = jnp.min(jnp.where(score == m, lane_or_row, size), axis=axis, keepdims=True)
        hit = lane_or_row == idx
        sel = sel | (hit & (r < n_valid))
        score = jnp.where(hit, -jnp.inf, score)
    return sel


def _moba_prompt_body(q_ref, k_ref, v_ref, o_ref, *, seq):
    tq = ATT_QTILE
    nblk = seq // MOBA_BLOCK
    i = pl.program_id(2)
    q = q_ref[0]
    k = k_ref[0]
    kmean = jnp.mean(k.reshape(nblk, MOBA_BLOCK, HEAD_DIM), axis=1)
    kmean = jnp.concatenate([kmean, jnp.zeros((LANES - nblk, HEAD_DIM), f32)], axis=0)
    lane = lax.broadcasted_iota(jnp.int32, (tq, LANES), 1)
    score = _dot_nt(q, kmean, precision=lax.Precision.HIGHEST)
    score = jnp.where(lane < i, score, -jnp.inf)
    sel = _topk_mask(score, lane, i, 1, LANES)
    key = lax.broadcasted_iota(jnp.int32, (LANES, seq), 1)
    blk_row = lax.broadcasted_iota(jnp.int32, (LANES, seq), 0)
    expand = jnp.where(key // MOBA_BLOCK == blk_row, 1.0, 0.0).astype(bf16)
    sel_keys = _dot(jnp.where(sel, 1.0, 0.0).astype(bf16), expand) > 0.5
    kpos = lax.broadcasted_iota(jnp.int32, (tq, seq), 1)
    qpos = i * tq + lax.broadcasted_iota(jnp.int32, (tq, seq), 0)
    own = (kpos // MOBA_BLOCK == i) & (kpos <= qpos)
    s = _dot_nt(q.astype(bf16), k.astype(bf16)) * (HEAD_DIM ** -0.5)
    s = jnp.where(sel_keys | own, s, -jnp.inf)
    p = jnp.exp(s - jnp.max(s, axis=-1, keepdims=True))
    p = p / jnp.sum(p, axis=-1, keepdims=True)
    o_ref[0] = _dot(p.astype(bf16), v_ref[0].astype(bf16)).astype(o_ref.dtype)


def moba_prompt(z, B, L):
    tq = ATT_QTILE
    assert L % tq == 0 and tq == MOBA_BLOCK
    qb, kb, vb = Z_QB // HEAD_DIM, Z_KB // HEAD_DIM, Z_VB // HEAD_DIM
    return pl.pallas_call(
        functools.partial(_moba_prompt_body, seq=L),
        out_shape=jax.ShapeDtypeStruct((B, L, MOBA_W), bf16),
        grid=(B, MOBA_HEADS, L // tq),
        in_specs=[pl.BlockSpec((1, tq, HEAD_DIM), lambda b, h, i: (b, i, qb + h)),
                  pl.BlockSpec((1, L, HEAD_DIM), lambda b, h, i: (b, 0, kb + h)),
                  pl.BlockSpec((1, L, HEAD_DIM), lambda b, h, i: (b, 0, vb + h))],
        out_specs=pl.BlockSpec((1, tq, HEAD_DIM), lambda b, h, i: (b, i, h)),
        compiler_params=_cparams("parallel", "parallel", "parallel"),
        name="moba_prompt",
    )(z, z, z)


def _diff_lambda(dl_ref, lam_init):
    dl = dl_ref[...]
    a = jnp.sum(dl[0:1] * dl[1:2], axis=-1, keepdims=True)
    b = jnp.sum(dl[2:3] * dl[3:4], axis=-1, keepdims=True)
    return jnp.exp(a) - jnp.exp(b) + lam_init


def _diff_prompt_body(q_ref, k_ref, v_ref, dl_ref, nw_ref, o_ref, *, seq, lam_init):
    tq = ATT_QTILE
    i = pl.program_id(2)
    lam = _diff_lambda(dl_ref, lam_init)
    kpos = lax.broadcasted_iota(jnp.int32, (tq, seq), 1)
    qpos = i * tq + lax.broadcasted_iota(jnp.int32, (tq, seq), 0)
    causal = kpos <= qpos
    ps = []
    for m in range(2):
        cols = slice(m * HEAD_DIM, (m + 1) * HEAD_DIM)
        s = _dot_nt(q_ref[0, :, cols].astype(bf16), k_ref[0, :, cols].astype(bf16)) * (HEAD_DIM ** -0.5)
        s = jnp.where(causal, s, -jnp.inf)
        p = jnp.exp(s - jnp.max(s, axis=-1, keepdims=True))
        ps.append(p / jnp.sum(p, axis=-1, keepdims=True))
    w = ps[0] - lam * ps[1]
    o = _dot(w.astype(bf16), v_ref[0].astype(bf16))
    y = o * lax.rsqrt(jnp.mean(o * o, axis=-1, keepdims=True) + RMS_EPS) * nw_ref[...]
    o_ref[0] = (y * (1.0 - lam_init)).astype(o_ref.dtype)


def diff_prompt(z, B, L, diff_lambda, norm_w, lam_init):
    tq = ATT_QTILE
    hw = 2 * HEAD_DIM
    qb, kb, vb = Z_QC // hw, Z_KC // hw, Z_VC // hw
    return pl.pallas_call(
        functools.partial(_diff_prompt_body, seq=L, lam_init=lam_init),
        out_shape=jax.ShapeDtypeStruct((B, L, DIFF_W), bf16),
        grid=(B, DIFF_HEADS, L // tq),
        in_specs=[pl.BlockSpec((1, tq, hw), lambda b, h, i: (b, i, qb + h)),
                  pl.BlockSpec((1, L, hw), lambda b, h, i: (b, 0, kb + h)),
                  pl.BlockSpec((1, L, hw), lambda b, h, i: (b, 0, vb + h)),
                  pl.BlockSpec((4, HEAD_DIM), lambda b, h, i: (0, 0)),
                  pl.BlockSpec((1, hw), lambda b, h, i: (0, 0))],
        out_specs=pl.BlockSpec((1, tq, hw), lambda b, h, i: (b, i, h)),
        compiler_params=_cparams("parallel", "parallel", "parallel"),
        name="diff_prompt",
    )(z, z, z, diff_lambda, norm_w.reshape(1, hw))


def _decode_attn_body(pt_ref, qx_ref, knew_ref, vnew_ref, dl_ref, nw_ref, *rest,
                      mode, n_steps, n_new, lam_init):
    pp = DEC_PAGES_PER_STEP
    k_refs = rest[:pp]
    v_refs = rest[pp:2 * pp]
    o_ref = rest[2 * pp]
    s_scr, bsum_scr, bmax_scr, sel_scr, inv_scr, pnew_scr, acc_scr = rest[2 * pp + 1:]
    ph = pl.program_id(1)
    g = pl.program_id(2)
    rows_step = pp * PAGE_SIZE
    blk_step = rows_step // MOBA_BLOCK
    nblk = n_steps * blk_step
    n_kh = MOBA_HEADS
    hv = acc_scr.shape[0]
    kh_per_v = n_kh // hv
    scale = HEAD_DIM ** -0.5

    def scores(load_head):
        acc = None
        for hp in range(n_kh // 2):
            lhs = jnp.concatenate([load_head(2 * hp), load_head(2 * hp + 1)], axis=1).astype(bf16)
            d = _dot_nt(lhs, qx_ref[0, hp])
            acc = d if acc is None else acc + d
        return acc

    @pl.when(ph == 0)
    def _():
        s = scores(lambda h: jnp.concatenate([k_refs[j][:, h, :] for j in range(pp)], axis=0))
        s_scr[pl.ds(pl.multiple_of(g * rows_step, rows_step), rows_step), :] = s
        for j in range(blk_step):
            blk = s[j * MOBA_BLOCK:(j + 1) * MOBA_BLOCK]
            bsum_scr[pl.ds(g * blk_step + j, 1), :] = jnp.mean(blk, axis=0, keepdims=True)
            bmax_scr[pl.ds(g * blk_step + j, 1), :] = jnp.max(blk, axis=0, keepdims=True)

    @pl.when((ph == 1) & (g == 0))
    def _():
        col = lax.broadcasted_iota(jnp.int32, (LANES, LANES), 1)
        krow = lax.broadcasted_iota(jnp.int32, (LANES, LANES), 0)
        new_ok = (krow < n_new) & (krow <= col % n_new)
        s_new = jnp.where(new_ok, scores(lambda h: knew_ref[0, :, h, :]) * scale, -jnp.inf)
        bmax = bmax_scr[...]
        if mode == "moba":
            brow = lax.broadcasted_iota(jnp.int32, (nblk, LANES), 0)
            sel = _topk_mask(bsum_scr[...], brow, nblk, 0, nblk)
            sel_scr[...] = jnp.where(sel, 1.0, 0.0)
            bmax = jnp.where(sel, bmax, -jnp.inf)
        m = jnp.maximum(jnp.max(bmax, axis=0, keepdims=True) * scale, jnp.max(s_new, axis=0, keepdims=True))

        def exp_block(b, tot):
            r0 = pl.multiple_of(b * MOBA_BLOCK, MOBA_BLOCK)
            s = s_scr[pl.ds(r0, MOBA_BLOCK), :] * scale
            if mode == "moba":
                s = jnp.where(sel_scr[pl.ds(b, 1), :] > 0.5, s, -jnp.inf)
            p = jnp.exp(s - m)
            s_scr[pl.ds(r0, MOBA_BLOCK), :] = p
            return tot + jnp.sum(p, axis=0, keepdims=True)

        p_new = jnp.exp(s_new - m)
        tot = lax.fori_loop(0, nblk, exp_block, jnp.sum(p_new, axis=0, keepdims=True))
        inv = 1.0 / tot
        inv_scr[...] = jnp.broadcast_to(inv, inv_scr.shape)
        pnew_scr[...] = p_new * inv
        acc_scr[...] = jnp.zeros_like(acc_scr)

    def accumulate(p, load_vhead):
        pt = p.T.astype(bf16)
        for h in range(hv):
            acc_scr[h] += _dot(pt, load_vhead(h).astype(bf16))

    @pl.when(ph == 1)
    def _():
        p = s_scr[pl.ds(pl.multiple_of(g * rows_step, rows_step), rows_step), :] * inv_scr[0:1, :]
        accumulate(p, lambda h: jnp.concatenate([v_refs[j][:, h, :] for j in range(pp)], axis=0))

    @pl.when((ph == 1) & (g == n_steps - 1))
    def _():
        accumulate(pnew_scr[...], lambda h: vnew_ref[0, :, h, :])
        dv = acc_scr.shape[2]
        rows_h = kh_per_v * n_new
        if mode == "diff":
            lam = _diff_lambda(dl_ref, lam_init)
        for h in range(hv):
            a = acc_scr[h, h * rows_h:(h + 1) * rows_h, :]
            if mode == "diff":
                o = a[:n_new] - lam * a[n_new:]
                y = o * lax.rsqrt(jnp.mean(o * o, axis=-1, keepdims=True) + RMS_EPS) * nw_ref[...]
                a = y * (1.0 - lam_init)
            o_ref[0, :, h * dv:(h + 1) * dv] = a.astype(o_ref.dtype)


def decode_attention(mode, layer, page_table, q, k_new, v_new, cache_k, cache_v,
                     diff_lambda, norm_w, lam_init):
    B, n_new, n_kh, _ = q.shape
    hv, dv = v_new.shape[2], v_new.shape[3]
    n_pages = page_table.shape[1]
    pp = DEC_PAGES_PER_STEP
    assert n_pages % pp == 0 and n_new == 8 and n_kh == MOBA_HEADS
    n_steps = n_pages // pp
    n_keys = n_pages * PAGE_SIZE
    eye = jnp.eye(n_kh, dtype=f32)
    qx = jnp.einsum("bqhd,hg->bhgqd", q, eye).reshape(B, n_kh, n_kh * n_new, HEAD_DIM)
    qx = jnp.pad(qx, ((0, 0), (0, 0), (0, LANES - n_kh * n_new), (0, 0)))
    qx = qx.reshape(B, n_kh // 2, 2, LANES, HEAD_DIM).transpose(0, 1, 3, 2, 4).reshape(B, n_kh // 2, LANES, 2 * HEAD_DIM)
    qx = qx.astype(bf16)
    pad_rows = lambda t: jnp.pad(t, ((0, 0), (0, PAGE_SIZE - n_new), (0, 0), (0, 0)))
    knew, vnew = pad_rows(k_new), pad_rows(v_new)

    def k_map(j):
        return lambda b, ph, g, pt: (layer, pt[b, (g * (1 - ph) + (n_steps - 1) * ph) * pp + j], 0, 0, 0)

    def v_map(j):
        return lambda b, ph, g, pt: (layer, pt[b, g * ph * pp + j], 0, 0, 0)

    per_b = lambda *shape: pl.BlockSpec((1,) + shape, lambda b, ph, g, pt: (b,) + (0,) * len(shape))
    grid_spec = pltpu.PrefetchScalarGridSpec(
        num_scalar_prefetch=1,
        grid=(B, 2, n_steps),
        in_specs=[per_b(n_kh // 2, LANES, 2 * HEAD_DIM),
                  per_b(PAGE_SIZE, n_kh, HEAD_DIM),
                  per_b(PAGE_SIZE, hv, dv),
                  pl.BlockSpec((4, HEAD_DIM), lambda b, ph, g, pt: (0, 0)),
                  pl.BlockSpec((1, dv), lambda b, ph, g, pt: (0, 0))]
                 + [pl.BlockSpec((None, None, PAGE_SIZE, n_kh, HEAD_DIM), k_map(j)) for j in range(pp)]
                 + [pl.BlockSpec((None, None, PAGE_SIZE, hv, dv), v_map(j)) for j in range(pp)],
        out_specs=per_b(n_new, hv * dv),
        scratch_shapes=[pltpu.VMEM((n_keys, LANES), f32),
                        pltpu.VMEM((n_keys // MOBA_BLOCK, LANES), f32),
                        pltpu.VMEM((n_keys // MOBA_BLOCK, LANES), f32),
                        pltpu.VMEM((n_keys // MOBA_BLOCK, LANES), f32),
                        pltpu.VMEM((8, LANES), f32),
                        pltpu.VMEM((PAGE_SIZE, LANES), f32),
                        pltpu.VMEM((hv, LANES, dv), f32)],
    )
    return pl.pallas_call(
        functools.partial(_decode_attn_body, mode=mode, n_steps=n_steps, n_new=n_new, lam_init=lam_init),
        out_shape=jax.ShapeDtypeStruct((B, n_new, hv * dv), f32),
        grid_spec=grid_spec,
        compiler_params=_cparams("arbitrary", "arbitrary", "arbitrary"),
        name="decode_attn_" + mode,
    )(page_table, qx, knew, vnew, diff_lambda, norm_w.reshape(1, dv),
      *([cache_k] * pp), *([cache_v] * pp))


def _branch_body(oa_ref, ob_ref, oc_ref, wa_ref, wb_ref, wc_ref, ba_ref, bb_ref, bc_ref, o_ref):
    m = (_sigmoid(ba_ref[...]) * _dot(oa_ref[...], wa_ref[...])
         + _sigmoid(bb_ref[...]) * _dot(ob_ref[...], wb_ref[...])
         + _sigmoid(bc_ref[...]) * _dot(oc_ref[...], wc_ref[...]))
    o_ref[...] = m.astype(o_ref.dtype)


def branch_merge(oa, ob, oc, wa, wb, wc, z, tm):
    M = oa.shape[0]
    D = wa.shape[1]
    tn = 512
    assert Z_BR % tn == 0 and D % tn == 0 and M % tm == 0
    br0 = Z_BR // tn
    nd = D // tn
    act = lambda w: pl.BlockSpec((tm, w), lambda i, j: (i, 0))
    wsp = lambda kdim: pl.BlockSpec((kdim, tn), lambda i, j: (0, j))
    brs = lambda n: pl.BlockSpec((tm, tn), lambda i, j: (i, br0 + n * nd + j))
    return pl.pallas_call(
        _branch_body,
        out_shape=jax.ShapeDtypeStruct((M, D), bf16),
        grid=(M // tm, nd),
        in_specs=[act(oa.shape[1]), act(ob.shape[1]), act(oc.shape[1]),
                  wsp(wa.shape[0]), wsp(wb.shape[0]), wsp(wc.shape[0]),
                  brs(0), brs(1), brs(2)],
        out_specs=pl.BlockSpec((tm, tn), lambda i, j: (i, j)),
        compiler_params=_cparams("parallel", "parallel"),
        name="branch_merge",
    )(oa, ob, oc, wa, wb, wc, z, z, z)


def _proj_resid_body(a_ref, w_ref, x_ref, g_ref, o_ref):
    g = g_ref[0] if len(g_ref.shape) == 3 else g_ref[...]
    o_ref[...] = x_ref[...] + g * _dot(a_ref[...], w_ref[...])


def proj_residual(a, w, x, gate, rows_per_seq, tm):
    M, K = a.shape
    D = w.shape[1]
    tn = 512
    if rows_per_seq % tm == 0:
        per = rows_per_seq // tm
        g_arg = gate.reshape(gate.shape[0], 1, D)
        g_spec = pl.BlockSpec((1, 1, tn), lambda i, j: (i // per, 0, j))
    else:
        g_arg = jnp.repeat(gate, rows_per_seq, axis=0)
        g_spec = pl.BlockSpec((tm, tn), lambda i, j: (i, j))
    return pl.pallas_call(
        _proj_resid_body,
        out_shape=jax.ShapeDtypeStruct((M, D), f32),
        grid=(M // tm, D // tn),
        in_specs=[pl.BlockSpec((tm, K), lambda i, j: (i, 0)),
                  pl.BlockSpec((K, tn), lambda i, j: (0, j)),
                  pl.BlockSpec((tm, tn), lambda i, j: (i, j)),
                  g_spec],
        out_specs=pl.BlockSpec((tm, tn), lambda i, j: (i, j)),
        compiler_params=_cparams("parallel", "parallel"),
        name="proj_residual",
    )(a, w, x, g_arg)


def _router_body(h_ref, w_ref, b_ref, idx_ref, gate_ref):
    logits = _dot(h_ref[...], w_ref[...].astype(bf16)) + b_ref[...]
    lane = lax.broadcasted_iota(jnp.int32, logits.shape, 1)
    logits = jnp.where(lane < N_EXPERTS, logits, -jnp.inf)
    idx_out = jnp.zeros(logits.shape, jnp.int32)
    val_out = jnp.full(logits.shape, -jnp.inf, f32)
    for r in range(TOP_K):
        m = jnp.max(logits, axis=-1, keepdims=True)
        idx = jnp.min(jnp.where(logits == m, lane, LANES), axis=-1, keepdims=True)
        idx_out = jnp.where(lane == r, idx, idx_out)
        val_out = jnp.where(lane == r, m, val_out)
        logits = jnp.where(lane == idx, -jnp.inf, logits)
    e = jnp.exp(val_out - jnp.max(val_out, axis=-1, keepdims=True))
    idx_ref[...] = idx_out
    gate_ref[...] = e / jnp.sum(e, axis=-1, keepdims=True)


def router(h, w_router, b_router, tm):
    T, D = h.shape
    w = jnp.pad(w_router, ((0, 0), (0, LANES - N_EXPERTS)))
    b = jnp.pad(b_router, (0, LANES - N_EXPERTS)).reshape(1, LANES)
    return pl.pallas_call(
        _router_body,
        out_shape=[jax.ShapeDtypeStruct((T, LANES), jnp.int32), jax.ShapeDtypeStruct((T, LANES), f32)],
        grid=(T // tm,),
        in_specs=[pl.BlockSpec((tm, D), lambda i: (i, 0)),
                  pl.BlockSpec((D, LANES), lambda i: (0, 0)),
                  pl.BlockSpec((1, LANES), lambda i: (0, 0))],
        out_specs=[pl.BlockSpec((tm, LANES), lambda i: (i, 0))] * 2,
        compiler_params=_cparams("parallel"),
        name="router",
    )(h, w, b)


def _cast_rows(src_ref, dst_ref):
    step = 512
    for r0 in range(0, src_ref.shape[0], step):
        dst_ref[r0:r0 + step, :] = src_ref[r0:r0 + step, :].astype(bf16)


def _moe_up_body(ie, ij, ib, ifirst, ivalid, x_ref, wg_ref, wu_ref, bg_ref, bu_ref, o_ref, wg_s, wu_s):
    i = pl.program_id(0)

    @pl.when(ifirst[i] == 1)
    def _():
        _cast_rows(wg_ref, wg_s)
        _cast_rows(wu_ref, wu_s)

    @pl.when(ivalid[i] == 1)
    def _():
        x = x_ref[...]
        gate = jnp.minimum(_dot(x, wg_s[...]) + bg_ref[...], SWIGLU_LIMIT)
        up = jnp.clip(_dot(x, wu_s[...]) + bu_ref[...], -SWIGLU_LIMIT, SWIGLU_LIMIT)
        o_ref[...] = ((up + 1.0) * gate * _sigmoid(SWIGLU_ALPHA * gate)).astype(o_ref.dtype)

    @pl.when(ivalid[i] == 0)
    def _():
        o_ref[...] = jnp.zeros_like(o_ref)


def _moe_down_body(ie, ij, ib, ifirst, ivalid, a_ref, w_ref, b_ref, sg_ref, o_ref, w_s):
    i = pl.program_id(0)

    @pl.when(ifirst[i] == 1)
    def _():
        _cast_rows(w_ref, w_s)

    @pl.when(ivalid[i] == 1)
    def _():
        o_ref[...] = (_dot(a_ref[...], w_s[...]) + b_ref[...]) * sg_ref[...]

    @pl.when(ivalid[i] == 0)
    def _():
        o_ref[...] = jnp.zeros_like(o_ref)


def _moe_items(nb_e, blk_start, n_blocks_max, n_j):
    n_items = n_blocks_max * n_j
    total = jnp.sum(nb_e) * n_j
    item_end = jnp.cumsum(nb_e) * n_j
    i = jnp.arange(n_items, dtype=jnp.int32)
    i_c = jnp.minimum(i, total - 1)
    e = jnp.minimum(jnp.searchsorted(item_end, i_c, side="right"), N_EXPERTS - 1).astype(jnp.int32)
    r = i_c - blk_start[e] * n_j
    nb = jnp.maximum(nb_e[e], 1)
    valid = (i < total).astype(jnp.int32)
    j = (r // nb).astype(jnp.int32)
    b = jnp.where(valid == 1, blk_start[e] + r % nb, n_blocks_max).astype(jnp.int32)
    first = ((r % nb == 0) & (valid == 1)).astype(jnp.int32)
    return e, j, b, first, valid


def moe_ffn(h, layer, w_router, b_router, w_gu, b_gu, w_dn, b_dn):
    T, D = h.shape
    de = w_dn.shape[2]
    bm = MOE_BM
    idx128, gate128 = router(h, w_router[layer], b_router[layer], tm=T // 12)
    top_idx, gates = idx128[:, :TOP_K], gate128[:, :TOP_K]
    A = T * TOP_K
    flat_e = top_idx.reshape(A)
    order = jnp.argsort(flat_e).astype(jnp.int32)
    se = flat_e[order]
    counts = jnp.zeros((N_EXPERTS,), jnp.int32).at[flat_e].add(1)
    nb_e = (counts + bm - 1) // bm
    blk_start = jnp.cumsum(nb_e) - nb_e
    grp_start = jnp.cumsum(counts) - counts
    dest = (blk_start[se] * bm + jnp.arange(A, dtype=jnp.int32) - grp_start[se]).astype(jnp.int32)
    n_blocks_max = (A + N_EXPERTS * (bm - 1)) // bm
    P = (n_blocks_max + 1) * bm
    slot_tok = jnp.full((P,), T, jnp.int32).at[dest].set(order // TOP_K)
    slot_gate = jnp.zeros((P,), f32).at[dest].set(gates.reshape(A)[order])
    inv = jnp.zeros((A,), jnp.int32).at[order].set(dest).reshape(T, TOP_K)
    h_pad = jnp.concatenate([h, jnp.zeros((1, D), h.dtype)], axis=0)
    xb = h_pad[slot_tok]

    tn1 = 512
    nj1 = de // tn1
    items = _moe_items(nb_e, blk_start, n_blocks_max, nj1)
    b_gu4 = b_gu.reshape(b_gu.shape[0], N_EXPERTS, 1, 2 * de)
    act = pl.pallas_call(
        _moe_up_body,
        out_shape=jax.ShapeDtypeStruct((P, de), bf16),
        grid_spec=pltpu.PrefetchScalarGridSpec(
            num_scalar_prefetch=5,
            grid=(n_blocks_max * nj1,),
            in_specs=[pl.BlockSpec((bm, D), lambda i, e, j, b, f, v: (b[i], 0)),
                      pl.BlockSpec((None, None, D, tn1), lambda i, e, j, b, f, v: (layer, e[i], 0, j[i])),
                      pl.BlockSpec((None, None, D, tn1), lambda i, e, j, b, f, v: (layer, e[i], 0, j[i] + nj1)),
                      pl.BlockSpec((None, None, 1, tn1), lambda i, e, j, b, f, v: (layer, e[i], 0, j[i])),
                      pl.BlockSpec((None, None, 1, tn1), lambda i, e, j, b, f, v: (layer, e[i], 0, j[i] + nj1))],
            out_specs=pl.BlockSpec((bm, tn1), lambda i, e, j, b, f, v: (b[i], j[i])),
            scratch_shapes=[pltpu.VMEM((D, tn1), bf16), pltpu.VMEM((D, tn1), bf16)]),
        compiler_params=_cparams("arbitrary"),
        name="moe_up",
    )(*items, xb, w_gu, w_gu, b_gu4, b_gu4)

    tn2 = 1024
    nj2 = D // tn2
    items2 = _moe_items(nb_e, blk_start, n_blocks_max, nj2)
    b_dn4 = b_dn.reshape(b_dn.shape[0], N_EXPERTS, 1, D)
    yb = pl.pallas_call(
        _moe_down_body,
        out_shape=jax.ShapeDtypeStruct((P, D), f32),
        grid_spec=pltpu.PrefetchScalarGridSpec(
            num_scalar_prefetch=5,
            grid=(n_blocks_max * nj2,),
            in_specs=[pl.BlockSpec((bm, de), lambda i, e, j, b, f, v: (b[i], 0)),
                      pl.BlockSpec((None, None, de, tn2), lambda i, e, j, b, f, v: (layer, e[i], 0, j[i])),
                      pl.BlockSpec((None, None, 1, tn2), lambda i, e, j, b, f, v: (layer, e[i], 0, j[i])),
                      pl.BlockSpec((bm, 1), lambda i, e, j, b, f, v: (b[i], 0))],
            out_specs=pl.BlockSpec((bm, tn2), lambda i, e, j, b, f, v: (b[i], j[i])),
            scratch_shapes=[pltpu.VMEM((de, tn2), bf16)]),
        compiler_params=_cparams("arbitrary"),
        name="moe_down",
    )(*items2, act, w_dn, b_dn4, slot_gate.reshape(P, 1))
    return yb[inv[:, 0]] + yb[inv[:, 1]] + yb[inv[:, 2]] + yb[inv[:, 3]]


def _mixer(l, x, mod, past, W):
    B, L, D = x.shape
    M = B * L
    prompt = past is None
    sh1, sc1, g1 = mod[0], mod[1], mod[2]
    h = rms_norm_mod(x, W["norm_mix"][l], sc1, sh1).reshape(M, D)
    tm = 1024 if M % 1024 == 0 else M
    z = matmul(h, W["w_in_main"][l], tm=tm, tn=512 if prompt else 1536, name="in_proj")
    zs = matmul(h, W["w_in_small"][l], tm=tm, tn=2 * LANES, name="in_proj_small")
    nz = z.shape[1]
    z3 = z.reshape(B, L, nz)

    if prompt:
        zqkv, zs3, lp = z3, zs.reshape(B, L, 2 * LANES), L
        buf8 = jnp.zeros((B, 8, GDN_CONV_W), f32)
        state0 = jnp.zeros((B, GDN_HEADS, HEAD_DIM, HEAD_DIM), f32)
        gate_arr, gate_col0 = z3, Z_GATE_A
    else:
        lp = GDN_TILE
        rpad = ((0, 0), (0, lp - L), (0, 0))
        zqkv = jnp.pad(z3[:, :, :GDN_CONV_W], rpad)
        zs3 = jnp.pad(zs.reshape(B, L, 2 * LANES), rpad)
        buf8 = jnp.pad(past["state_conv"][l], ((0, 0), (8 - (GDN_CONV - 1), 0), (0, 0)))
        state0 = past["state_gdn"][l]
        gate_arr, gate_col0 = jnp.pad(z3[:, :, Z_GATE_A:Z_GATE_A + GDN_W], rpad), 0
    qa, ka, va, gcum, beta = gdn_prep(zqkv, zs3, buf8, W["conv_w"][l], W["a_log"][l], W["dt_bias"][l], L)
    w1, w2, qk, qg, kd = gdn_intra(qa, ka, va, gcum, beta)
    o_a, gdn_new = gdn_seq(w1, w2, qg, kd, qk, gcum, gate_arr, gate_col0, state0, W["gdn_norm_w"][l])
    o_a = o_a[:, :L].reshape(M, GDN_W)
    conv_new = (jnp.concatenate([buf8[:, 8 - (GDN_CONV - 1):], z3[:, :, :GDN_CONV_W]], axis=1))[:, -(GDN_CONV - 1):]

    kb = z3[:, :, Z_KB:Z_KB + MOBA_W].reshape(B, L, MOBA_HEADS, HEAD_DIM)
    vb = z3[:, :, Z_VB:Z_VB + MOBA_W].reshape(B, L, MOBA_HEADS, HEAD_DIM)
    kc = z3[:, :, Z_KC:Z_KC + DIFF_W].reshape(B, L, 2 * DIFF_HEADS, HEAD_DIM)
    vc = z3[:, :, Z_VC:Z_VC + DIFF_W].reshape(B, L, DIFF_HEADS, 2 * HEAD_DIM)
    lam_init = 0.8 - 0.6 * math.exp(-0.3 * l)
    if prompt:
        o_b = moba_prompt(z3, B, L)
        o_c = diff_prompt(z3, B, L, W["diff_lambda"][l], W["diff_norm_w"][l], lam_init)
    else:
        qb = z3[:, :, Z_QB:Z_QB + MOBA_W].reshape(B, L, MOBA_HEADS, HEAD_DIM)
        qc = z3[:, :, Z_QC:Z_QC + DIFF_W].reshape(B, L, 2 * DIFF_HEADS, HEAD_DIM)
        pt = past["page_table"]
        o_b = decode_attention("moba", l, pt, qb, kb, vb, past["cache_moba_k"], past["cache_moba_v"],
                               W["diff_lambda"][l], jnp.ones((HEAD_DIM,), f32), lam_init)
        o_c = decode_attention("diff", l, pt, qc, kc, vc, past["cache_diff_k"], past["cache_diff_v"],
                               W["diff_lambda"][l], W["diff_norm_w"][l], lam_init)
    o_b = o_b.reshape(M, MOBA_W).astype(bf16)
    o_c = o_c.reshape(M, DIFF_W).astype(bf16)

    wa, wb, wc = W["w_branch_split"][l]
    merged = branch_merge(o_a, o_b, o_c, wa, wb, wc, z, tm)
    x_new = proj_residual(merged, W["w_out_bf"][l], x.reshape(M, D), g1, L, tm).reshape(B, L, D)
    return x_new, (gdn_new, conv_new, kb, vb, kc, vc)


def kernel(x_prompt, x_sample, cache_moba_k, cache_moba_v, cache_diff_k, cache_diff_v, state_gdn, state_conv,
           page_table, c_prompt, c_sample, w_mod, b_mod, norm_mix, norm_ffn, w_in, conv_w, a_log, dt_bias,
           gdn_norm_w, diff_lambda, diff_norm_w, w_branch, w_out, w_router, b_router, w_gu, b_gu, w_dn, b_dn,
           norm_out):
    depth = w_in.shape[0]
    D = x_prompt.shape[-1]
    Bp, Lp, _ = x_prompt.shape
    Bs, Ls, _ = x_sample.shape
    n_dec = 2 * GDN_HEADS
    small0 = GDN_CONV_W

    w_in_main = [jnp.concatenate([w_in[l][:, :small0], w_in[l][:, small0 + n_dec:]], axis=1).astype(bf16)
                 for l in range(depth)]
    w_in_small = [jnp.zeros((D, 2 * LANES), f32)
                  .at[:, :GDN_HEADS].set(w_in[l][:, small0:small0 + GDN_HEADS])
                  .at[:, LANES:LANES + GDN_HEADS].set(w_in[l][:, small0 + GDN_HEADS:small0 + n_dec])
                  for l in range(depth)]
    w_branch_split = [(w_branch[l][:GDN_W].astype(bf16), w_branch[l][GDN_W:GDN_W + MOBA_W].astype(bf16),
                       w_branch[l][GDN_W + MOBA_W:].astype(bf16)) for l in range(depth)]
    W = dict(norm_mix=norm_mix, norm_ffn=norm_ffn, w_in_main=w_in_main, w_in_small=w_in_small, conv_w=conv_w,
             a_log=a_log, dt_bias=dt_bias, gdn_norm_w=gdn_norm_w, diff_lambda=diff_lambda, diff_norm_w=diff_norm_w,
             w_branch_split=w_branch_split, w_out_bf=[w_out[l].astype(bf16) for l in range(depth)])
    past = dict(page_table=page_table, state_gdn=state_gdn, state_conv=state_conv, cache_moba_k=cache_moba_k,
                cache_moba_v=cache_moba_v, cache_diff_k=cache_diff_k, cache_diff_v=cache_diff_v)

    c_all = jnp.concatenate([c_prompt, c_sample], axis=0)
    n_c = c_all.shape[0]
    c_act = jnp.pad(c_all * jax.nn.sigmoid(c_all), ((0, -n_c % 16), (0, 0)))
    xp, xs = x_prompt, x_sample
    rows_p, rows_s = [], []
    Tp, Ts = Bp * Lp, Bs * Ls
    for l in range(depth):
        mod = matmul(c_act, w_mod, layer=l, bias=b_mod[l], tm=c_act.shape[0], tn=1024, name="adaln_mod")
        mod = jnp.split(mod[:n_c], N_MOD, axis=-1)
        mod_p = [m[:Bp] for m in mod]
        mod_s = [m[Bp:] for m in mod]
        xp, rp = _mixer(l, xp, mod_p, None, W)
        xs, rs = _mixer(l, xs, mod_s, past, W)
        rows_p.append(rp)
        rows_s.append(rs)
        hp = rms_norm_mod(xp, norm_ffn[l], mod_p[4], mod_p[3]).reshape(Tp, D)
        hs = rms_norm_mod(xs, norm_ffn[l], mod_s[4], mod_s[3]).reshape(Ts, D)
        y = moe_ffn(jnp.concatenate([hp, hs], axis=0), l, w_router, b_router, w_gu, b_gu, w_dn, b_dn)
        xp = xp + mod_p[5][:, None, :] * y[:Tp].reshape(Bp, Lp, D)
        xs = xs + mod_s[5][:, None, :] * y[Tp:].reshape(Bs, Ls, D)
    yp = rms_norm_mod(xp, norm_out, out_dtype=f32)
    ys = rms_norm_mod(xs, norm_out, out_dtype=f32)
    stack = lambda rows, i: jnp.stack([r[i] for r in rows])
    return ((yp, ys) + tuple(stack(rows_p, i) for i in range(6)) + tuple(stack(rows_s, i) for i in range(6)))
```

```python
import functools
import math

import jax
import jax.numpy as jnp
from jax import lax
from jax.experimental import pallas as pl
from jax.experimental.pallas import tpu as pltpu

f32 = jnp.float32
bf16 = jnp.bfloat16

HEAD_DIM = 128
GDN_HEADS = 12
GDN_CONV = 4
GDN_CHUNK = 64
GDN_W = GDN_HEADS * HEAD_DIM
GDN_CONV_W = 3 * GDN_W
MOBA_HEADS = 10
MOBA_BLOCK = 256
MOBA_TOPK = 3
MOBA_W = MOBA_HEADS * HEAD_DIM
DIFF_HEADS = 5
DIFF_W = 2 * DIFF_HEADS * HEAD_DIM
N_EXPERTS = 32
TOP_K = 4
SWIGLU_LIMIT = 7.0
SWIGLU_ALPHA = 1.702
N_MOD = 6
RMS_EPS = 1e-6
L2_EPS = 1e-6
PAGE_SIZE = 128

LANES = 128
GDN_TILE = 256
GDN_HEAD_GROUP = 4
GDN_INTRA_HEADS = 2
ATT_QTILE = 256
DEC_PAGES_PER_STEP = 4
MOE_BM = 512
MOE_W_SPLIT = 2
VMEM_LIMIT_BYTES =56 * 1024 * 1024

Z_GATE_A = GDN_CONV_W
Z_QB = Z_GATE_A + GDN_W
Z_KB = Z_QB + MOBA_W
Z_VB = Z_KB + MOBA_W
Z_QC = Z_VB + MOBA_W
Z_KC = Z_QC + DIFF_W
Z_VC = Z_KC + DIFF_W
Z_BR = Z_VC + DIFF_W


def _cparams(*sem):
    return pltpu.CompilerParams(dimension_semantics=sem, vmem_limit_bytes=VMEM_LIMIT_BYTES)


def _dot(a, b):
    return jnp.dot(a, b, preferred_element_type=f32)


def _dot_nt(a, b, precision=None):
    return lax.dot_general(a, b, (((1,), (1,)), ((), ())), preferred_element_type=f32, precision=precision)


def _dot_tn(a, b):
    return lax.dot_general(a, b, (((0,), (0,)), ((), ())), preferred_element_type=f32)


def _sigmoid(x):
    return 1.0 / (1.0 + jnp.exp(-x))


def _norm_body(x_ref, w_ref, *rest, modulate):
    o_ref = rest[-1]
    x = x_ref[0]
    y = x * lax.rsqrt(jnp.mean(x * x, axis=-1, keepdims=True) + RMS_EPS) * w_ref[...]
    if modulate:
        sc_ref, sh_ref = rest[0], rest[1]
        y = y * (1.0 + sc_ref[0]) + sh_ref[0]
    o_ref[0] = y.astype(o_ref.dtype)


def rms_norm_mod(x, w, sc=None, sh=None, out_dtype=bf16):
    B, L, D = x.shape
    tl = min(L, 512)
    modulate = sc is not None
    in_specs = [pl.BlockSpec((1, tl, D), lambda b, i: (b, i, 0)),
                pl.BlockSpec((1, D), lambda b, i: (0, 0))]
    args = [x, w.reshape(1, D)]
    if modulate:
        in_specs += [pl.BlockSpec((1, 1, D), lambda b, i: (b, 0, 0))] * 2
        args += [sc.reshape(B, 1, D), sh.reshape(B, 1, D)]
    return pl.pallas_call(
        functools.partial(_norm_body, modulate=modulate),
        out_shape=jax.ShapeDtypeStruct((B, L, D), out_dtype),
        grid=(B, L // tl),
        in_specs=in_specs,
        out_specs=pl.BlockSpec((1, tl, D), lambda b, i: (b, i, 0)),
        compiler_params=_cparams("parallel", "parallel"),
        name="rms_norm_mod",
    )(*args)


def _mm_bias_body(a_ref, *rest, k_split):
    w_refs, bias_ref, o_ref = rest[:k_split], rest[k_split], rest[k_split + 1]
    kq = w_refs[0].shape[0]
    acc = bias_ref[...]
    for q, w_ref in enumerate(w_refs):
        acc = acc + _dot(a_ref[:, q * kq:(q + 1) * kq].astype(bf16), w_ref[...].astype(bf16))
    o_ref[...] = acc


def matmul_bias(a, w, layer, bias, *, tn, k_split, name):
    M, K = a.shape
    N = w.shape[-1]
    assert N % tn == 0 and K % k_split == 0
    w_specs = [pl.BlockSpec((None, K // k_split, tn), functools.partial(lambda q, j: (layer, q, j), q))
               for q in range(k_split)]
    return pl.pallas_call(
        functools.partial(_mm_bias_body, k_split=k_split),
        out_shape=jax.ShapeDtypeStruct((M, N), f32),
        grid=(N // tn,),
        in_specs=[pl.BlockSpec((M, K), lambda j: (0, 0))] + w_specs + [pl.BlockSpec((1, tn), lambda j: (0, j))],
        out_specs=pl.BlockSpec((M, tn), lambda j: (0, j)),
        compiler_params=_cparams("parallel"),
        name=name,
    )(a, *([w] * k_split), bias.reshape(1, N))


def _mm_nt_body(a_ref, w_ref, o_ref):
    o_ref[...] = _dot_nt(a_ref[...].astype(bf16), w_ref[...].astype(bf16)).astype(o_ref.dtype)


def matmul_nt(a, wt, *, tm, tn, out_dtype=f32, name="matmul_nt"):
    M, K = a.shape
    N = wt.shape[0]
    assert M % tm == 0 and N % tn == 0
    return pl.pallas_call(
        _mm_nt_body,
        out_shape=jax.ShapeDtypeStruct((M, N), out_dtype),
        grid=(M // tm, N // tn),
        in_specs=[pl.BlockSpec((tm, K), lambda i, j: (i, 0)),
                  pl.BlockSpec((tn, K), lambda i, j: (j, 0))],
        out_specs=pl.BlockSpec((tm, tn), lambda i, j: (i, j)),
        compiler_params=_cparams("parallel", "parallel"),
        name=name,
    )(a, wt)


def _softplus(x):
    return jnp.maximum(x, 0.0) + jnp.log1p(jnp.exp(-jnp.abs(x)))


def _gdn_prep_body(x_ref, prev_ref, buf_ref, cw_ref, zs_ref, al_ref, dtb_ref,
                   q_ref, k_ref, v_ref, g_ref, beta_ref, *, tl, valid_len):
    i = pl.program_id(1)
    row = lax.broadcasted_iota(jnp.int32, (tl, 1), 0)
    valid = (i * tl + row) < valid_len
    row8 = lax.broadcasted_iota(jnp.int32, (8, LANES), 0)
    first = i == 0
    outs = (q_ref, k_ref, v_ref)
    for part in range(3):
        for h in range(GDN_HEADS):
            c0 = part * GDN_W + h * HEAD_DIM
            cols = slice(c0, c0 + HEAD_DIM)
            x = x_ref[0, :, cols]
            prev = jnp.where(first, buf_ref[0, :, cols], prev_ref[0, :, cols])
            y = x * cw_ref[GDN_CONV - 1:GDN_CONV, cols]
            for s in range(1, GDN_CONV):
                xs = pltpu.roll(x, s, axis=0)
                top = jnp.where(row8 < s, pltpu.roll(prev, s, axis=0), xs[:8])
                xs = jnp.concatenate([top, xs[8:]], axis=0)
                y = y + xs * cw_ref[GDN_CONV - 1 - s:GDN_CONV - s, cols]
            y = y * _sigmoid(y)
            if part < 2:
                y = y * lax.rsqrt(jnp.sum(y * y, axis=-1, keepdims=True) + L2_EPS)
            if part == 0:
                y = y * (HEAD_DIM ** -0.5)
            outs[part][0, :, h * HEAD_DIM:(h + 1) * HEAD_DIM] = jnp.where(valid, y, 0.0)
    zs = zs_ref[0]
    g = -jnp.exp(al_ref[...]) * _softplus(zs[:, :LANES] + dtb_ref[...])
    g = jnp.where(valid, g, 0.0)
    beta_ref[0] = jnp.where(valid, _sigmoid(zs[:, LANES:]), 0.0)
    in_chunk = row % GDN_CHUNK
    s = 1
    while s < GDN_CHUNK:
        g = g + jnp.where(in_chunk >= s, pltpu.roll(g, s, axis=0), 0.0)
        s *= 2
    g_ref[0] = g


def gdn_prep(zqkv, zs, buf8, conv_w, a_log, dt_bias, valid_len, tl):
    B, L, _ = zqkv.shape
    assert L % tl == 0 and tl % GDN_CHUNK == 0
    nprev = tl // 8
    pad = lambda t: jnp.pad(t.astype(f32), (0, LANES - GDN_HEADS)).reshape(1, LANES)
    row_spec = lambda w: pl.BlockSpec((1, tl, w), lambda b, i: (b, i, 0))
    outs = pl.pallas_call(
        functools.partial(_gdn_prep_body, tl=tl, valid_len=valid_len),
        out_shape=[jax.ShapeDtypeStruct((B, L, GDN_W), f32)] * 3 + [jax.ShapeDtypeStruct((B, L, LANES), f32)] * 2,
        grid=(B, L // tl),
        in_specs=[row_spec(GDN_CONV_W),
                  pl.BlockSpec((1, 8, GDN_CONV_W), lambda b, i: (b, jnp.maximum(i * nprev - 1, 0), 0)),
                  pl.BlockSpec((1, 8, GDN_CONV_W), lambda b, i: (b, 0, 0)),
                  pl.BlockSpec((GDN_CONV, GDN_CONV_W), lambda b, i: (0, 0)),
                  row_spec(2 * LANES),
                  pl.BlockSpec((1, LANES), lambda b, i: (0, 0)),
                  pl.BlockSpec((1, LANES), lambda b, i: (0, 0))],
        out_specs=[row_spec(GDN_W)] * 3 + [row_spec(LANES)] * 2,
        compiler_params=_cparams("parallel", "parallel"),
        name="gdn_prep",
    )(zqkv, zqkv, buf8, conv_w, zs, pad(a_log), pad(dt_bias))
    return outs


def _split_bf16(a):
    hi = a.astype(bf16)
    return hi, (a - hi.astype(f32)).astype(bf16)


def _dot_split(a, b):
    return _dot(a[0], b[0]) + _dot(a[0], b[1]) + _dot(a[1], b[0])


def _gdn_intra_body(q_ref, k_ref, v_ref, g_ref, beta_ref,
                    w1_ref, w2_ref, qk_ref, qg_ref, kd_ref, *, tl, nil):
    nh = GDN_INTRA_HEADS
    hg = pl.program_id(1)
    lane = lax.broadcasted_iota(jnp.int32, (tl, LANES), 1)
    r = lax.broadcasted_iota(jnp.int32, (tl, tl), 0)
    c = lax.broadcasted_iota(jnp.int32, (tl, tl), 1)
    same = (r // GDN_CHUNK) == (c // GDN_CHUNK)
    eye = r == c
    is_last = c == (r // GDN_CHUNK) * GDN_CHUNK + (GDN_CHUNK - 1)
    heads = []
    for hh in range(nh):
        cols = slice(hh * HEAD_DIM, (hh + 1) * HEAD_DIM)
        h = hg * nh + hh
        gcol = jnp.sum(jnp.where(lane == h, g_ref[0], 0.0), axis=-1, keepdims=True)
        bcol = jnp.sum(jnp.where(lane == h, beta_ref[0], 0.0), axis=-1, keepdims=True)
        grow = jnp.sum(jnp.where(eye, gcol, 0.0), axis=0, keepdims=True)
        decay = jnp.exp(jnp.where(same & (r >= c), gcol - grow, -jnp.inf))
        k = k_ref[0, :, cols]
        q = q_ref[0, :, cols]
        kb = k.astype(bf16)
        n = jnp.where(same & (r > c), -(bcol * _dot_nt(kb, kb) * decay), 0.0)
        gam = jnp.exp(gcol)
        qk_ref[0, hh] = _dot_nt(q.astype(bf16), kb) * decay
        qg_ref[0, :, cols] = q * gam
        glast = jnp.sum(jnp.where(is_last, grow, 0.0), axis=-1, keepdims=True)
        kd_ref[0, :, cols] = k * jnp.exp(glast - gcol)
        rhs = jnp.concatenate([bcol * v_ref[0, :, cols], (bcol * gam) * k], axis=1)
        heads.append(dict(t=jnp.where(eye, 1.0, 0.0) + n, p=_split_bf16(n), rhs=_split_bf16(rhs)))
    span = 2
    while span < nil:
        for st in heads:
            st["p"] = _split_bf16(_dot_split(st["p"], st["p"]))
        for st in heads:
            st["t"] = st["t"] + _dot_split(_split_bf16(st["t"]), st["p"])
        span *= 2
    ws = [_dot_split(_split_bf16(st["t"]), st["rhs"]) for st in heads]
    w1_ref[0] = jnp.concatenate([w[:, :HEAD_DIM] for w in ws], axis=1)
    w2_ref[0] = jnp.concatenate([w[:, HEAD_DIM:] for w in ws], axis=1)


def gdn_intra(q, k, v, g, beta, tl, nil):
    B, L, _ = q.shape
    nh = GDN_INTRA_HEADS
    head_spec = pl.BlockSpec((1, tl, nh * HEAD_DIM), lambda b, h, i: (b, i, h))
    all_spec = pl.BlockSpec((1, tl, LANES), lambda b, h, i: (b, i, 0))
    wide = jax.ShapeDtypeStruct((B, L, GDN_W), f32)
    return pl.pallas_call(
        functools.partial(_gdn_intra_body, tl=tl, nil=nil),
        out_shape=[wide, wide, jax.ShapeDtypeStruct((B, GDN_HEADS, L, tl), f32), wide, wide],
        grid=(B, GDN_HEADS // nh, L // tl),
        in_specs=[head_spec, head_spec, head_spec, all_spec, all_spec],
        out_specs=[head_spec, head_spec,
                   pl.BlockSpec((1, nh, tl, tl), lambda b, h, i: (b, h, i, 0)),
                   head_spec, head_spec],
        compiler_params=_cparams("parallel", "parallel", "parallel"),
        name="gdn_intra",
    )(q, k, v, g, beta)


def _gdn_seq_body(w1_ref, w2_ref, qg_ref, kd_ref, qk_ref, g_ref, gate_ref, s0_ref, nw_ref,
                  o_ref, so_ref, s_scr, u_scr, *, tl):
    hg = pl.program_id(1)
    ti = pl.program_id(2)

    @pl.when(ti == 0)
    def _():
        s_scr[...] = s0_ref[0]

    u_scr[...] = jnp.zeros_like(u_scr)
    lane = lax.broadcasted_iota(jnp.int32, (1, LANES), 1)
    for ci in range(tl // GDN_CHUNK):
        rows = slice(ci * GDN_CHUNK, (ci + 1) * GDN_CHUNK)
        last = ci * GDN_CHUNK + GDN_CHUNK - 1
        g_last_row = g_ref[0, last:last + 1, :]
        for hh in range(GDN_HEAD_GROUP):
            cols = slice(hh * HEAD_DIM, (hh + 1) * HEAD_DIM)
            s = s_scr[hh]
            sb = s.astype(bf16)
            u = w1_ref[0, rows, cols] - _dot(w2_ref[0, rows, cols].astype(bf16), sb)
            u_scr[hh, rows, :] = u
            o = _dot(qg_ref[0, rows, cols].astype(bf16), sb) + _dot(qk_ref[0, hh, rows, :].astype(bf16),
                                                                  u_scr[hh].astype(bf16))
            g_last = jnp.sum(jnp.where(lane == hg * GDN_HEAD_GROUP + hh, g_last_row, 0.0), axis=-1, keepdims=True)
            s_scr[hh] = jnp.exp(g_last) * s + _dot_tn(kd_ref[0, rows, cols].astype(bf16), u.astype(bf16))
            y = o * lax.rsqrt(jnp.mean(o * o, axis=-1, keepdims=True) + RMS_EPS) * nw_ref[...]
            gate = gate_ref[0, rows, cols]
            o_ref[0, rows, cols] = (y * (gate * _sigmoid(gate))).astype(o_ref.dtype)

    @pl.when(ti == pl.num_programs(2) - 1)
    def _():
        so_ref[0] = s_scr[...]


def gdn_seq(w1, w2, qg, kd, qk, g, gate_arr, gate_col0, state0, norm_w, tl):
    B, L, _ = w1.shape
    gw = GDN_HEAD_GROUP * HEAD_DIM
    assert gate_col0 % gw == 0
    gcb = gate_col0 // gw
    grp = pl.BlockSpec((1, tl, gw), lambda b, hg, i: (b, i, hg))
    return pl.pallas_call(
        functools.partial(_gdn_seq_body, tl=tl),
        out_shape=[jax.ShapeDtypeStruct((B, L, GDN_W), bf16),
                   jax.ShapeDtypeStruct((B, GDN_HEADS, HEAD_DIM, HEAD_DIM), f32)],
        grid=(B, GDN_HEADS // GDN_HEAD_GROUP, L // tl),
        in_specs=[grp, grp, grp, grp,
                  pl.BlockSpec((1, GDN_HEAD_GROUP, tl, tl), lambda b, hg, i: (b, hg, i, 0)),
                  pl.BlockSpec((1, tl, LANES), lambda b, hg, i: (b, i, 0)),
                  pl.BlockSpec((1, tl, gw), lambda b, hg, i: (b, i, gcb + hg)),
                  pl.BlockSpec((1, GDN_HEAD_GROUP, HEAD_DIM, HEAD_DIM), lambda b, hg, i: (b, hg, 0, 0)),
                  pl.BlockSpec((1, HEAD_DIM), lambda b, hg, i: (0, 0))],
        out_specs=[grp,
                   pl.BlockSpec((1, GDN_HEAD_GROUP, HEAD_DIM, HEAD_DIM), lambda b, hg, i: (b, hg, 0, 0))],
        scratch_shapes=[pltpu.VMEM((GDN_HEAD_GROUP, HEAD_DIM, HEAD_DIM), f32),
                        pltpu.VMEM((GDN_HEAD_GROUP, tl, HEAD_DIM), f32)],
        compiler_params=_cparams("parallel", "parallel", "arbitrary"),
        name="gdn_seq",
    )(w1, w2, qg, kd, qk, g, gate_arr, state0, norm_w.reshape(1, HEAD_DIM))


def _topk_mask(score, lane_or_row, n_valid, axis, size):
    sel = jnp.zeros(score.shape, jnp.bool_)
    for r in range(MOBA_TOPK):
        m = jnp.max(score, axis=axis, keepdims=True)
        idx = jnp.min(jnp.where(score == m, lane_or_row, size), axis=axis, keepdims=True)
        hit = lane_or_row == idx
        sel = sel | (hit & (r < n_valid))
        score = jnp.where(hit, -jnp.inf, score)
    return sel


def _moba_prompt_body(q_ref, k_ref, v_ref, o_ref, ko_ref, vo_ref, kmean_scr, *, seq):
    nblk = seq // MOBA_BLOCK

    @pl.when(pl.program_id(2) == 0)
    def _():
        k = k_ref[0]
        ko_ref[0, 0] = k
        vo_ref[0, 0] = v_ref[0]
        kmean = jnp.mean(k.reshape(nblk, MOBA_BLOCK, HEAD_DIM), axis=1)
        kmean_scr[...] = jnp.concatenate([kmean, jnp.zeros((LANES - nblk, HEAD_DIM), f32)], axis=0)

    for n in range(nblk):
        pl.when(pl.program_id(2) == n)(
            functools.partial(_moba_prompt_tile, q_ref, k_ref, v_ref, o_ref, kmean_scr, n))


def _moba_prompt_tile(q_ref, k_ref, v_ref, o_ref, kmean_scr, i):
    tq = ATT_QTILE
    seq = (i + 1) * MOBA_BLOCK
    q = q_ref[0]
    k = k_ref[0, :seq, :]
    lane = lax.broadcasted_iota(jnp.int32, (tq, LANES), 1)
    score = _dot_nt(q, kmean_scr[...], precision=lax.Precision.HIGHEST)
    score = jnp.where(lane < i, score, -jnp.inf)
    sel = _topk_mask(score, lane, i, 1, LANES)
    key = lax.broadcasted_iota(jnp.int32, (LANES, seq), 1)
    blk_row = lax.broadcasted_iota(jnp.int32, (LANES, seq), 0)
    member = jnp.where(key // MOBA_BLOCK == blk_row, 1.0, 0.0).astype(bf16)
    sel_keys = _dot(jnp.where(sel, 1.0, 0.0).astype(bf16), member) > 0.5
    kpos = lax.broadcasted_iota(jnp.int32, (tq, seq), 1)
    qpos = i * tq + lax.broadcasted_iota(jnp.int32, (tq, seq), 0)
    own = (kpos // MOBA_BLOCK == i) & (kpos <= qpos)
    s = _dot_nt(q.astype(bf16), k.astype(bf16)) * (HEAD_DIM ** -0.5)
    s = jnp.where(sel_keys | own, s, -jnp.inf)
    p = jnp.exp(s - jnp.max(s, axis=-1, keepdims=True))
    p = p / jnp.sum(p, axis=-1, keepdims=True)
    o_ref[0] = _dot(p.astype(bf16), v_ref[0, :seq, :].astype(bf16)).astype(o_ref.dtype)


def moba_prompt(z, B, L):
    tq = ATT_QTILE
    assert L % tq == 0 and tq == MOBA_BLOCK
    qb, kb, vb = Z_QB // HEAD_DIM, Z_KB // HEAD_DIM, Z_VB // HEAD_DIM
    rows = jax.ShapeDtypeStruct((B, MOBA_HEADS, L, HEAD_DIM), f32)
    rows_spec = pl.BlockSpec((1, 1, L, HEAD_DIM), lambda b, h, i: (b, h, 0, 0))
    return pl.pallas_call(
        functools.partial(_moba_prompt_body, seq=L),
        out_shape=[jax.ShapeDtypeStruct((B, L, MOBA_W), bf16), rows, rows],
        grid=(B, MOBA_HEADS, L // tq),
        in_specs=[pl.BlockSpec((1, tq, HEAD_DIM), lambda b, h, i: (b, i, qb + h)),
                  pl.BlockSpec((1, L, HEAD_DIM), lambda b, h, i: (b, 0, kb + h)),
                  pl.BlockSpec((1, L, HEAD_DIM), lambda b, h, i: (b, 0, vb + h))],
        out_specs=[pl.BlockSpec((1, tq, HEAD_DIM), lambda b, h, i: (b, i, h)), rows_spec, rows_spec],
        scratch_shapes=[pltpu.VMEM((LANES, HEAD_DIM), f32)],
        compiler_params=_cparams("parallel", "parallel", "arbitrary"),
        name="moba_prompt",
    )(z, z, z)


def _diff_lambda(dl_ref, lam_init):
    dl = dl_ref[...]
    a = jnp.sum(dl[0:1] * dl[1:2], axis=-1, keepdims=True)
    b = jnp.sum(dl[2:3] * dl[3:4], axis=-1, keepdims=True)
    return jnp.exp(a) - jnp.exp(b) + lam_init


def _diff_prompt_body(q_ref, k_ref, v_ref, dl_ref, nw_ref, o_ref, ko_ref, vo_ref, *, seq, lam_init):
    @pl.when(pl.program_id(2) == 0)
    def _():
        ko_ref[0, 0] = k_ref[0, :, :HEAD_DIM]
        ko_ref[0, 1] = k_ref[0, :, HEAD_DIM:]
        vo_ref[0, 0] = v_ref[0]

    for n in range(seq // ATT_QTILE):
        pl.when(pl.program_id(2) == n)(
            functools.partial(_diff_prompt_tile, q_ref, k_ref, v_ref, dl_ref, nw_ref, o_ref, n, lam_init))


def _diff_prompt_tile(q_ref, k_ref, v_ref, dl_ref, nw_ref, o_ref, i, lam_init):
    tq = ATT_QTILE
    seq = (i + 1) * tq
    lam = _diff_lambda(dl_ref, lam_init)
    kpos = lax.broadcasted_iota(jnp.int32, (tq, seq), 1)
    qpos = i * tq + lax.broadcasted_iota(jnp.int32, (tq, seq), 0)
    causal = kpos <= qpos
    ps = []
    for m in range(2):
        cols = slice(m * HEAD_DIM, (m + 1) * HEAD_DIM)
        s = _dot_nt(q_ref[0, :, cols].astype(bf16), k_ref[0, :seq, cols].astype(bf16)) * (HEAD_DIM ** -0.5)
        s = jnp.where(causal, s, -jnp.inf)
        p = jnp.exp(s - jnp.max(s, axis=-1, keepdims=True))
        ps.append(p / jnp.sum(p, axis=-1, keepdims=True))
    w = ps[0] - lam * ps[1]
    o = _dot(w.astype(bf16), v_ref[0, :seq, :].astype(bf16))
    y = o * lax.rsqrt(jnp.mean(o * o, axis=-1, keepdims=True) + RMS_EPS) * nw_ref[...]
    o_ref[0] = (y * (1.0 - lam_init)).astype(o_ref.dtype)


def diff_prompt(z, B, L, diff_lambda, norm_w, lam_init):
    tq = ATT_QTILE
    hw = 2 * HEAD_DIM
    qb, kb, vb = Z_QC // hw, Z_KC // hw, Z_VC // hw
    return pl.pallas_call(
        functools.partial(_diff_prompt_body, seq=L, lam_init=lam_init),
        out_shape=[jax.ShapeDtypeStruct((B, L, DIFF_W), bf16),
                   jax.ShapeDtypeStruct((B, 2 * DIFF_HEADS, L, HEAD_DIM), f32),
                   jax.ShapeDtypeStruct((B, DIFF_HEADS, L, hw), f32)],
        grid=(B, DIFF_HEADS, L // tq),
        in_specs=[pl.BlockSpec((1, tq, hw), lambda b, h, i: (b, i, qb + h)),
                  pl.BlockSpec((1, L, hw), lambda b, h, i: (b, 0, kb + h)),
                  pl.BlockSpec((1, L, hw), lambda b, h, i: (b, 0, vb + h)),
                  pl.BlockSpec((4, HEAD_DIM), lambda b, h, i: (0, 0)),
                  pl.BlockSpec((1, hw), lambda b, h, i: (0, 0))],
        out_specs=[pl.BlockSpec((1, tq, hw), lambda b, h, i: (b, i, h)),
                   pl.BlockSpec((1, 2, L, HEAD_DIM), lambda b, h, i: (b, h, 0, 0)),
                   pl.BlockSpec((1, 1, L, hw), lambda b, h, i: (b, h, 0, 0))],
        compiler_params=_cparams("parallel", "parallel", "arbitrary"),
        name="diff_prompt",
    )(z, z, z, diff_lambda, norm_w.reshape(1, hw))


def _decode_attn_body(pt_ref, qx_ref, knew_ref, vnew_ref, dl_ref, nw_ref, *rest,
                      mode, n_steps, n_new, lam_init):
    pp = DEC_PAGES_PER_STEP
    k_refs = rest[:pp]
    v_refs = rest[pp:2 * pp]
    o_ref = rest[2 * pp]
    s_scr, bsum_scr, bmax_scr, sel_scr, inv_scr, pnew_scr, acc_scr = rest[2 * pp + 1:]
    ph = pl.program_id(1)
    g = pl.program_id(2)
    rows_step = pp * PAGE_SIZE
    blk_step = rows_step // MOBA_BLOCK
    nblk = n_steps * blk_step
    n_kh = MOBA_HEADS
    hv = acc_scr.shape[0]
    kh_per_v = n_kh // hv
    scale = HEAD_DIM ** -0.5

    def scores(load_head):
        acc = None
        for hp in range(n_kh // 2):
            lhs = jnp.concatenate([load_head(2 * hp), load_head(2 * hp + 1)], axis=1).astype(bf16)
            d = _dot_nt(lhs, qx_ref[0, hp])
            acc = d if acc is None else acc + d
        return acc

    @pl.when(ph == 0)
    def _():
        s = scores(lambda h: jnp.concatenate([k_refs[j][h] for j in range(pp)], axis=0))
        s_scr[pl.ds(pl.multiple_of(g * rows_step, rows_step), rows_step), :] = s
        for j in range(blk_step):
            blk = s[j * MOBA_BLOCK:(j + 1) * MOBA_BLOCK]
            bsum_scr[pl.ds(g * blk_step + j, 1), :] = jnp.mean(blk, axis=0, keepdims=True)
            bmax_scr[pl.ds(g * blk_step + j, 1), :] = jnp.max(blk, axis=0, keepdims=True)

    @pl.when((ph == 1) & (g == 0))
    def _():
        col = lax.broadcasted_iota(jnp.int32, (LANES, LANES), 1)
        krow = lax.broadcasted_iota(jnp.int32, (LANES, LANES), 0)
        new_ok = (krow < n_new) & (krow <= col % n_new)
        s_new = jnp.where(new_ok, scores(lambda h: knew_ref[0, h]) * scale, -jnp.inf)
        bmax = bmax_scr[...]
        if mode == "moba":
            brow = lax.broadcasted_iota(jnp.int32, (nblk, LANES), 0)
            sel = _topk_mask(bsum_scr[...], brow, nblk, 0, nblk)
            sel_scr[...] = jnp.where(sel, 1.0, 0.0)
            bmax = jnp.where(sel, bmax, -jnp.inf)
        m = jnp.maximum(jnp.max(bmax, axis=0, keepdims=True) * scale, jnp.max(s_new, axis=0, keepdims=True))

        def exp_block(b, tot):
            r0 = pl.multiple_of(b * MOBA_BLOCK, MOBA_BLOCK)
            s = s_scr[pl.ds(r0, MOBA_BLOCK), :] * scale
            if mode == "moba":
                s = jnp.where(sel_scr[pl.ds(b, 1), :] > 0.5, s, -jnp.inf)
            p = jnp.exp(s - m)
            s_scr[pl.ds(r0, MOBA_BLOCK), :] = p
            return tot + jnp.sum(p, axis=0, keepdims=True)

        p_new = jnp.exp(s_new - m)
        tot = lax.fori_loop(0, nblk, exp_block, jnp.sum(p_new, axis=0, keepdims=True))
        inv = 1.0 / tot
        inv_scr[...] = jnp.broadcast_to(inv, inv_scr.shape)
        pnew_scr[...] = p_new * inv
        acc_scr[...] = jnp.zeros_like(acc_scr)

    def accumulate(p, load_vhead):
        pt = p.T.astype(bf16)
        for h in range(hv):
            acc_scr[h] += _dot(pt, load_vhead(h).astype(bf16))

    @pl.when(ph == 1)
    def _():
        p = s_scr[pl.ds(pl.multiple_of(g * rows_step, rows_step), rows_step), :] * inv_scr[0:1, :]
        accumulate(p, lambda h: jnp.concatenate([v_refs[j][h] for j in range(pp)], axis=0))

    @pl.when((ph == 1) & (g == n_steps - 1))
    def _():
        accumulate(pnew_scr[...], lambda h: vnew_ref[0, h])
        dv = acc_scr.shape[2]
        rows_h = kh_per_v * n_new
        if mode == "diff":
            lam = _diff_lambda(dl_ref, lam_init)
        for h in range(hv):
            a = acc_scr[h, h * rows_h:(h + 1) * rows_h, :]
            if mode == "diff":
                o = a[:n_new] - lam * a[n_new:]
                y = o * lax.rsqrt(jnp.mean(o * o, axis=-1, keepdims=True) + RMS_EPS) * nw_ref[...]
                a = y * (1.0 - lam_init)
            o_ref[0, :, h * dv:(h + 1) * dv] = a.astype(o_ref.dtype)


def decode_attention(mode, layer, page_table, q, k_new, v_new, cache_k, cache_v,
                     diff_lambda, norm_w, lam_init):
    B, n_new, n_kh, _ = q.shape
    hv, dv = v_new.shape[2], v_new.shape[3]
    n_pages = page_table.shape[1]
    pp = DEC_PAGES_PER_STEP
    assert n_pages % pp == 0 and n_new == 8 and n_kh == MOBA_HEADS
    n_steps = n_pages // pp
    n_keys = n_pages * PAGE_SIZE
    eye = jnp.eye(n_kh, dtype=f32)
    qx = jnp.einsum("bqhd,hg->bhgqd", q, eye).reshape(B, n_kh, n_kh * n_new, HEAD_DIM)
    qx = jnp.pad(qx, ((0, 0), (0, 0), (0, LANES - n_kh * n_new), (0, 0)))
    qx = qx.reshape(B, n_kh // 2, 2, LANES, HEAD_DIM).transpose(0, 1, 3, 2, 4).reshape(B, n_kh // 2, LANES, 2 * HEAD_DIM)
    qx = qx.astype(bf16)
    pad_rows = lambda t: jnp.pad(t.transpose(0, 2, 1, 3), ((0, 0), (0, 0), (0, PAGE_SIZE - n_new), (0, 0)))
    knew, vnew = pad_rows(k_new), pad_rows(v_new)

    def k_map(j):
        return lambda b, ph, g, pt: (layer, pt[b, (g * (1 - ph) + (n_steps - 1) * ph) * pp + j], 0, 0, 0)

    def v_map(j):
        return lambda b, ph, g, pt: (layer, pt[b, g * ph * pp + j], 0, 0, 0)

    per_b = lambda *shape: pl.BlockSpec((1,) + shape, lambda b, ph, g, pt: (b,) + (0,) * len(shape))
    grid_spec = pltpu.PrefetchScalarGridSpec(
        num_scalar_prefetch=1,
        grid=(B, 2, n_steps),
        in_specs=[per_b(n_kh // 2, LANES, 2 * HEAD_DIM),
                  per_b(n_kh, PAGE_SIZE, HEAD_DIM),
                  per_b(hv, PAGE_SIZE, dv),
                  pl.BlockSpec((4, HEAD_DIM), lambda b, ph, g, pt: (0, 0)),
                  pl.BlockSpec((1, dv), lambda b, ph, g, pt: (0, 0))]
                 + [pl.BlockSpec((None, None, n_kh, PAGE_SIZE, HEAD_DIM), k_map(j)) for j in range(pp)]
                 + [pl.BlockSpec((None, None, hv, PAGE_SIZE, dv), v_map(j)) for j in range(pp)],
        out_specs=per_b(n_new, hv * dv),
        scratch_shapes=[pltpu.VMEM((n_keys, LANES), f32),
                        pltpu.VMEM((n_keys // MOBA_BLOCK, LANES), f32),
                        pltpu.VMEM((n_keys // MOBA_BLOCK, LANES), f32),
                        pltpu.VMEM((n_keys // MOBA_BLOCK, LANES), f32),
                        pltpu.VMEM((8, LANES), f32),
                        pltpu.VMEM((PAGE_SIZE, LANES), f32),
                        pltpu.VMEM((hv, LANES, dv), f32)],
    )
    return pl.pallas_call(
        functools.partial(_decode_attn_body, mode=mode, n_steps=n_steps, n_new=n_new, lam_init=lam_init),
        out_shape=jax.ShapeDtypeStruct((B, n_new, hv * dv), f32),
        grid_spec=grid_spec,
        compiler_params=_cparams("arbitrary", "arbitrary", "arbitrary"),
        name="decode_attn_" + mode,
    )(page_table, qx, knew, vnew, diff_lambda, norm_w.reshape(1, dv),
      *([cache_k] * pp), *([cache_v] * pp))


def _branch_body(oa_ref, ob_ref, oc_ref, wa_ref, wb_ref, wc_ref, ba_ref, bb_ref, bc_ref, o_ref):
    m = (_sigmoid(ba_ref[...]) * _dot(oa_ref[...], wa_ref[...])
         + _sigmoid(bb_ref[...]) * _dot(ob_ref[...], wb_ref[...])
         + _sigmoid(bc_ref[...]) * _dot(oc_ref[...], wc_ref[...]))
    o_ref[...] = m.astype(o_ref.dtype)


def branch_merge(oa, ob, oc, wa, wb, wc, z, tm):
    M = oa.shape[0]
    D = wa.shape[1]
    tn = 512
    assert Z_BR % tn == 0 and D % tn == 0 and M % tm == 0
    br0 = Z_BR // tn
    nd = D // tn
    act = lambda w: pl.BlockSpec((tm, w), lambda i, j: (i, 0))
    wsp = lambda kdim: pl.BlockSpec((kdim, tn), lambda i, j: (0, j))
    brs = lambda n: pl.BlockSpec((tm, tn), lambda i, j: (i, br0 + n * nd + j))
    return pl.pallas_call(
        _branch_body,
        out_shape=jax.ShapeDtypeStruct((M, D), bf16),
        grid=(M // tm, nd),
        in_specs=[act(oa.shape[1]), act(ob.shape[1]), act(oc.shape[1]),
                  wsp(wa.shape[0]), wsp(wb.shape[0]), wsp(wc.shape[0]),
                  brs(0), brs(1), brs(2)],
        out_specs=pl.BlockSpec((tm, tn), lambda i, j: (i, j)),
        compiler_params=_cparams("parallel", "parallel"),
        name="branch_merge",
    )(oa, ob, oc, wa, wb, wc, z, z, z)


def _proj_resid_body(a_ref, w_ref, x_ref, g_ref, o_ref):
    g = g_ref[0] if len(g_ref.shape) == 3 else g_ref[...]
    o_ref[...] = x_ref[...] + g * _dot(a_ref[...], w_ref[...])


def proj_residual(a, w, x, gate, rows_per_seq, tm):
    M, K = a.shape
    D = w.shape[1]
    tn = 512
    if rows_per_seq % tm == 0:
        per = rows_per_seq // tm
        g_arg = gate.reshape(gate.shape[0], 1, D)
        g_spec = pl.BlockSpec((1, 1, tn), lambda i, j: (i // per, 0, j))
    else:
        g_arg = jnp.repeat(gate, rows_per_seq, axis=0)
        g_spec = pl.BlockSpec((tm, tn), lambda i, j: (i, j))
    return pl.pallas_call(
        _proj_resid_body,
        out_shape=jax.ShapeDtypeStruct((M, D), f32),
        grid=(M // tm, D // tn),
        in_specs=[pl.BlockSpec((tm, K), lambda i, j: (i, 0)),
                  pl.BlockSpec((K, tn), lambda i, j: (0, j)),
                  pl.BlockSpec((tm, tn), lambda i, j: (i, j)),
                  g_spec],
        out_specs=pl.BlockSpec((tm, tn), lambda i, j: (i, j)),
        compiler_params=_cparams("parallel", "parallel"),
        name="proj_residual",
    )(a, w, x, g_arg)


def _router_body(h_ref, w_ref, b_ref, idx_ref, gate_ref):
    logits = _dot_nt(h_ref[...], w_ref[...].astype(bf16)) + b_ref[...]
    lane = lax.broadcasted_iota(jnp.int32, logits.shape, 1)
    logits = jnp.where(lane < N_EXPERTS, logits, -jnp.inf)
    idx_out = jnp.zeros(logits.shape, jnp.int32)
    val_out = jnp.full(logits.shape, -jnp.inf, f32)
    for r in range(TOP_K):
        m = jnp.max(logits, axis=-1, keepdims=True)
        idx = jnp.min(jnp.where(logits == m, lane, LANES), axis=-1, keepdims=True)
        idx_out = jnp.where(lane == r, idx, idx_out)
        val_out = jnp.where(lane == r, m, val_out)
        logits = jnp.where(lane == idx, -jnp.inf, logits)
    e = jnp.exp(val_out - jnp.max(val_out, axis=-1, keepdims=True))
    idx_ref[...] = idx_out
    gate_ref[...] = e / jnp.sum(e, axis=-1, keepdims=True)


def router(h, w_router, b_router, tm):
    T, D = h.shape
    w = jnp.pad(w_router.T, ((0, LANES - N_EXPERTS), (0, 0)))
    b = jnp.pad(b_router, (0, LANES - N_EXPERTS)).reshape(1, LANES)
    return pl.pallas_call(
        _router_body,
        out_shape=[jax.ShapeDtypeStruct((T, LANES), jnp.int32), jax.ShapeDtypeStruct((T, LANES), f32)],
        grid=(T // tm,),
        in_specs=[pl.BlockSpec((tm, D), lambda i: (i, 0)),
                  pl.BlockSpec((LANES, D), lambda i: (0, 0)),
                  pl.BlockSpec((1, LANES), lambda i: (0, 0))],
        out_specs=[pl.BlockSpec((tm, LANES), lambda i: (i, 0))] * 2,
        compiler_params=_cparams("parallel"),
        name="router",
    )(h, w, b)


def _cast_rows(src_ref, dst_ref):
    step = 512
    for r0 in range(0, src_ref.shape[0], step):
        dst_ref[r0:r0 + step, :] = src_ref[r0:r0 + step, :].astype(bf16)


def _cast_parts(part_refs, dst_ref):
    rows = part_refs[0].shape[0]
    for q, ref in enumerate(part_refs):
        _cast_rows(ref, dst_ref.at[q * rows:(q + 1) * rows, :])


def _moe_up_body(ie, ij, ib, ifirst, ivalid, x_ref, *rest):
    ns = MOE_W_SPLIT
    wg_refs, wu_refs = rest[:ns], rest[ns:2 * ns]
    bg_ref, bu_ref, o_ref, wg_s, wu_s = rest[2 * ns:]
    i = pl.program_id(0)

    @pl.when(ifirst[i] == 1)
    def _():
        _cast_parts(wg_refs, wg_s)
        _cast_parts(wu_refs, wu_s)

    def act(x):
        gate = jnp.minimum(_dot(x, wg_s[...]) + bg_ref[...], SWIGLU_LIMIT)
        up = jnp.clip(_dot(x, wu_s[...]) + bu_ref[...], -SWIGLU_LIMIT, SWIGLU_LIMIT)
        return ((up + 1.0) * gate * _sigmoid(SWIGLU_ALPHA * gate)).astype(o_ref.dtype)

    _moe_block_cases(ivalid[i], o_ref, lambda rows: act(x_ref[rows, :]))


def _moe_block_cases(kind, o_ref, compute):
    half = o_ref.shape[0] // 2

    @pl.when(kind == 1)
    def _():
        o_ref[...] = compute(slice(None))

    @pl.when(kind == 2)
    def _():
        o_ref[:half, :] = compute(slice(0, half))
        o_ref[half:, :] = jnp.zeros((half, o_ref.shape[1]), o_ref.dtype)

    @pl.when(kind == 0)
    def _():
        o_ref[...] = jnp.zeros_like(o_ref)


def _moe_down_body(ie, ij, ib, ifirst, ivalid, a_ref, *rest):
    ns = MOE_W_SPLIT
    w_refs = rest[:ns]
    b_ref, sg_ref, o_ref, w_s = rest[ns:]
    i = pl.program_id(0)

    @pl.when(ifirst[i] == 1)
    def _():
        _cast_parts(w_refs, w_s)

    _moe_block_cases(ivalid[i], o_ref,
                     lambda rows: (_dot(a_ref[rows, :], w_s[...]) + b_ref[...]) * sg_ref[rows, :])


def _moe_items(nb_e, blk_start, counts, bm, n_blocks_max, n_j):
    n_items = n_blocks_max * n_j
    total = jnp.sum(nb_e) * n_j
    item_end = jnp.cumsum(nb_e) * n_j
    i = jnp.arange(n_items, dtype=jnp.int32)
    i_c = jnp.minimum(i, total - 1)
    e = jnp.minimum(jnp.searchsorted(item_end, i_c, side="right"), N_EXPERTS - 1).astype(jnp.int32)
    r = i_c - blk_start[e] * n_j
    nb = jnp.maximum(nb_e[e], 1)
    valid = (i < total).astype(jnp.int32)
    j = (r // nb).astype(jnp.int32)
    b = jnp.where(valid == 1, blk_start[e] + r % nb, n_blocks_max).astype(jnp.int32)
    first = ((r % nb == 0) & (valid == 1)).astype(jnp.int32)
    rows_used = counts[e] - (r % nb) * bm
    kind = jnp.where(valid == 1, jnp.where(rows_used <= bm // 2, 2, 1), 0).astype(jnp.int32)
    return e, j, b, first, kind


def moe_ffn(h, layer, w_router, b_router, w_gu, b_gu, w_dn, b_dn):
    T, D = h.shape
    de = w_dn.shape[2]
    bm = MOE_BM
    idx128, gate128 = router(h, w_router[layer], b_router[layer], tm=T // 12)
    top_idx, gates = idx128[:, :TOP_K], gate128[:, :TOP_K]
    A = T * TOP_K
    flat_e = top_idx.reshape(A)
    onehot = (flat_e[:, None] == jnp.arange(N_EXPERTS, dtype=jnp.int32)[None, :]).astype(jnp.int32)
    running = jnp.cumsum(onehot, axis=0)
    counts = running[-1]
    rank = jnp.sum(running * onehot, axis=1) - 1
    nb_e = (counts + bm - 1) // bm
    blk_start = jnp.cumsum(nb_e) - nb_e
    dest = (blk_start[flat_e] * bm + rank).astype(jnp.int32)
    n_blocks_max = (A + N_EXPERTS * (bm - 1)) // bm
    P = (n_blocks_max + 1) * bm
    slot_tok = jnp.full((P,), T, jnp.int32).at[dest].set(jnp.arange(A, dtype=jnp.int32) // TOP_K)
    slot_gate = jnp.zeros((P,), f32).at[dest].set(gates.reshape(A))
    inv = dest.reshape(T, TOP_K)
    h_pad = jnp.concatenate([h, jnp.zeros((1, D), h.dtype)], axis=0)
    xb = h_pad[slot_tok]

    ns = MOE_W_SPLIT

    def w_part(kdim, tn, q, j0):
        return pl.BlockSpec((None, None, kdim // ns, tn), lambda i, e, j, b, f, v: (layer, e[i], q, j[i] + j0))

    tn1 = 512
    nj1 = de // tn1
    items =_moe_items(nb_e, blk_start, counts, bm, n_blocks_max, nj1)
    b_gu4 = b_gu.reshape(b_gu.shape[0], N_EXPERTS, 1, 2 * de)
    act = pl.pallas_call(
        _moe_up_body,
        out_shape=jax.ShapeDtypeStruct((P, de), bf16),
        grid_spec=pltpu.PrefetchScalarGridSpec(
            num_scalar_prefetch=5,
            grid=(n_blocks_max * nj1,),
            in_specs=[pl.BlockSpec((bm, D), lambda i, e, j, b, f, v: (b[i], 0))]
                     + [w_part(D, tn1, q, 0) for q in range(ns)] + [w_part(D, tn1, q, nj1) for q in range(ns)]
                     + [pl.BlockSpec((None, None, 1, tn1), lambda i, e, j, b, f, v: (layer, e[i], 0, j[i])),
                        pl.BlockSpec((None, None, 1, tn1), lambda i, e, j, b, f, v: (layer, e[i], 0, j[i] + nj1))],
            out_specs=pl.BlockSpec((bm, tn1), lambda i, e, j, b, f, v: (b[i], j[i])),
            scratch_shapes=[pltpu.VMEM((D, tn1), bf16), pltpu.VMEM((D, tn1), bf16)]),
        compiler_params=_cparams("arbitrary"),
        name="moe_up",
    )(*items, xb, *([w_gu] * (2 * ns)), b_gu4, b_gu4)

    tn2 = 1024
    nj2 = D // tn2
    items2 = _moe_items(nb_e, blk_start, counts, bm, n_blocks_max, nj2)
    b_dn4 = b_dn.reshape(b_dn.shape[0], N_EXPERTS, 1, D)
    yb = pl.pallas_call(
        _moe_down_body,
        out_shape=jax.ShapeDtypeStruct((P, D), f32),
        grid_spec=pltpu.PrefetchScalarGridSpec(
            num_scalar_prefetch=5,
            grid=(n_blocks_max * nj2,),
            in_specs=[pl.BlockSpec((bm, de), lambda i, e, j, b, f, v: (b[i], 0))]
                     + [w_part(de, tn2, q, 0) for q in range(ns)]
                     + [pl.BlockSpec((None, None, 1, tn2), lambda i, e, j, b, f, v: (layer, e[i], 0, j[i])),
                        pl.BlockSpec((bm, 1), lambda i, e, j, b, f, v: (b[i], 0))],
            out_specs=pl.BlockSpec((bm, tn2), lambda i, e, j, b, f, v: (b[i], j[i])),
            scratch_shapes=[pltpu.VMEM((de, tn2), bf16)]),
        compiler_params=_cparams("arbitrary"),
        name="moe_down",
    )(*items2, act, *([w_dn] * ns), b_dn4, slot_gate.reshape(P, 1))
    return yb[inv[:, 0]] + yb[inv[:, 1]] + yb[inv[:, 2]] + yb[inv[:, 3]]


def _mixer(l, x, mod, past, W):
    B, L, D = x.shape
    M = B * L
    prompt = past is None
    sh1, sc1, g1 = mod[0], mod[1], mod[2]
    h = rms_norm_mod(x, W["norm_mix"][l], sc1, sh1).reshape(M, D)
    tm = 1024 if M % 1024 == 0 else M
    z = matmul_nt(h, W["w_in_main"][l], tm=tm, tn=512 if prompt else 1536, name="in_proj")
    zs = matmul_nt(h, W["w_in_small"][l], tm=tm, tn=2 * LANES, name="in_proj_small")
    nz = z.shape[1]
    z3 = z.reshape(B, L, nz)

    if prompt:
        zqkv, zs3, tile = z3, zs.reshape(B, L, 2 * LANES), GDN_TILE
        buf8 = jnp.zeros((B, 8, GDN_CONV_W), f32)
        state0 = jnp.zeros((B, GDN_HEADS, HEAD_DIM, HEAD_DIM), f32)
        gate_arr, gate_col0 = z3, Z_GATE_A
    else:
        tile = lp = GDN_CHUNK
        assert L <= lp
        rpad = ((0, 0), (0, lp - L), (0, 0))
        zqkv = jnp.pad(z3[:, :, :GDN_CONV_W], rpad)
        zs3 = jnp.pad(zs.reshape(B, L, 2 * LANES), rpad)
        buf8 = jnp.pad(past["state_conv"][l], ((0, 0), (8 - (GDN_CONV - 1), 0), (0, 0)))
        state0 = past["state_gdn"][l]
        gate_arr, gate_col0 = jnp.pad(z3[:, :, Z_GATE_A:Z_GATE_A + GDN_W], rpad), 0
    qa, ka, va, gcum, beta = gdn_prep(zqkv, zs3, buf8, W["conv_w"][l], W["a_log"][l], W["dt_bias"][l], L, tile)
    w1, w2, qk, qg, kd = gdn_intra(qa, ka, va, gcum, beta, tile, min(GDN_CHUNK, L))
    o_a, gdn_new = gdn_seq(w1, w2, qg, kd, qk, gcum, gate_arr, gate_col0, state0, W["gdn_norm_w"][l], tile)
    o_a = o_a[:, :L].reshape(M, GDN_W)
    conv_new = (jnp.concatenate([buf8[:, 8 - (GDN_CONV - 1):], z3[:, :, :GDN_CONV_W]], axis=1))[:, -(GDN_CONV - 1):]

    lam_init = 0.8 - 0.6 * math.exp(-0.3 * l)
    if prompt:
        o_b, kb, vb = moba_prompt(z3, B, L)
        o_c, kc, vc = diff_prompt(z3, B, L, W["diff_lambda"][l], W["diff_norm_w"][l], lam_init)
        kb, vb, kc, vc = (t.transpose(0, 2, 1, 3) for t in (kb, vb, kc, vc))
    else:
        kb = z3[:, :, Z_KB:Z_KB + MOBA_W].reshape(B, L, MOBA_HEADS, HEAD_DIM)
        vb = z3[:, :, Z_VB:Z_VB + MOBA_W].reshape(B, L, MOBA_HEADS, HEAD_DIM)
        kc = z3[:, :, Z_KC:Z_KC + DIFF_W].reshape(B, L, 2 * DIFF_HEADS, HEAD_DIM)
        vc = z3[:, :, Z_VC:Z_VC + DIFF_W].reshape(B, L, DIFF_HEADS, 2 * HEAD_DIM)
        qb = z3[:, :, Z_QB:Z_QB + MOBA_W].reshape(B, L, MOBA_HEADS, HEAD_DIM)
        qc = z3[:, :, Z_QC:Z_QC + DIFF_W].reshape(B, L, 2 * DIFF_HEADS, HEAD_DIM)
        pt = past["page_table"]
        o_b = decode_attention("moba", l, pt, qb, kb, vb, past["cache_moba_k"], past["cache_moba_v"],
                               W["diff_lambda"][l], jnp.ones((HEAD_DIM,), f32), lam_init)
        o_c = decode_attention("diff", l, pt, qc, kc, vc, past["cache_diff_k"], past["cache_diff_v"],
                               W["diff_lambda"][l], W["diff_norm_w"][l], lam_init)
    o_b = o_b.reshape(M, MOBA_W).astype(bf16)
    o_c = o_c.reshape(M, DIFF_W).astype(bf16)

    wa, wb, wc = W["w_branch_split"][l]
    merged = branch_merge(o_a, o_b, o_c, wa, wb, wc, z, tm)
    x_new = proj_residual(merged, W["w_out_bf"][l], x.reshape(M, D), g1, L, tm).reshape(B, L, D)
    return x_new, (gdn_new, conv_new, kb, vb, kc, vc)


def kernel(x_prompt, x_sample, cache_moba_k, cache_moba_v, cache_diff_k, cache_diff_v, state_gdn, state_conv,
           page_table, c_prompt, c_sample, w_mod, b_mod, norm_mix, norm_ffn, w_in, conv_w, a_log, dt_bias,
           gdn_norm_w, diff_lambda, diff_norm_w, w_branch, w_out, w_router, b_router, w_gu, b_gu, w_dn, b_dn,
           norm_out):
    depth = w_in.shape[0]
    D = x_prompt.shape[-1]
    Bp, Lp, _ = x_prompt.shape
    Bs, Ls, _ = x_sample.shape
    n_dec = 2 * GDN_HEADS
    small0 = GDN_CONV_W

    w_in_t = [w_in[l].T for l in range(depth)]
    w_in_main = [jnp.concatenate([w_in_t[l][:small0], w_in_t[l][small0 + n_dec:]], axis=0).astype(bf16)
                 for l in range(depth)]
    w_in_small = [jnp.zeros((2 * LANES, D), f32)
                  .at[:GDN_HEADS].set(w_in_t[l][small0:small0 + GDN_HEADS])
                  .at[LANES:LANES + GDN_HEADS].set(w_in_t[l][small0 + GDN_HEADS:small0 + n_dec])
                  for l in range(depth)]
    w_branch_split = [(w_branch[l][:GDN_W].astype(bf16), w_branch[l][GDN_W:GDN_W + MOBA_W].astype(bf16),
                       w_branch[l][GDN_W + MOBA_W:].astype(bf16)) for l in range(depth)]
    W = dict(norm_mix=norm_mix, norm_ffn=norm_ffn, w_in_main=w_in_main, w_in_small=w_in_small, conv_w=conv_w,
             a_log=a_log, dt_bias=dt_bias, gdn_norm_w=gdn_norm_w, diff_lambda=diff_lambda, diff_norm_w=diff_norm_w,
             w_branch_split=w_branch_split, w_out_bf=[w_out[l].astype(bf16) for l in range(depth)])
    head_major = lambda c: c.transpose(0, 1, 3, 2, 4)
    past = dict(page_table=page_table, state_gdn=state_gdn, state_conv=state_conv,
                cache_moba_k=head_major(cache_moba_k), cache_moba_v=head_major(cache_moba_v),
                cache_diff_k=head_major(cache_diff_k), cache_diff_v=head_major(cache_diff_v))

    c_all = jnp.concatenate([c_prompt, c_sample], axis=0)
    n_c = c_all.shape[0]
    c_act = jnp.pad(c_all * jax.nn.sigmoid(c_all), ((0, -n_c % 16), (0, 0)))
    xp, xs = x_prompt, x_sample
    rows_p, rows_s = [], []
    Tp, Ts = Bp * Lp, Bs * Ls
    for l in range(depth):
        mod = matmul_bias(c_act, w_mod, l, b_mod[l], tn=1024, k_split=4, name="adaln_mod")
        mod = jnp.split(mod[:n_c], N_MOD, axis=-1)
        mod_p = [m[:Bp] for m in mod]
        mod_s = [m[Bp:] for m in mod]
        xp, rp = _mixer(l, xp, mod_p, None, W)
        xs, rs = _mixer(l, xs, mod_s, past, W)
        rows_p.append(rp)
        rows_s.append(rs)
        hp = rms_norm_mod(xp, norm_ffn[l], mod_p[4], mod_p[3]).reshape(Tp, D)
        hs = rms_norm_mod(xs, norm_ffn[l], mod_s[4], mod_s[3]).reshape(Ts, D)
        y = moe_ffn(jnp.concatenate([hp, hs], axis=0), l, w_router, b_router, w_gu, b_gu, w_dn, b_dn)
        xp = xp + mod_p[5][:, None, :] * y[:Tp].reshape(Bp, Lp, D)
        xs = xs + mod_s[5][:, None, :] * y[Tp:].reshape(Bs, Ls, D)
    yp = rms_norm_mod(xp, norm_out, out_dtype=f32)
    ys = rms_norm_mod(xs, norm_out, out_dtype=f32)
    stack = lambda rows, i: jnp.stack([r[i] for r in rows])
    return ((yp, ys) + tuple(stack(rows_p, i) for i in range(6)) + tuple(stack(rows_s, i) for i in range(6)))
```

```python
import functools
import math

import jax
import jax.numpy as jnp
from jax import lax
from jax.experimental import pallas as pl
from jax.experimental.pallas import tpu as pltpu

f32 = jnp.float32
bf16 = jnp.bfloat16

HEAD_DIM = 128
GDN_HEADS = 12
GDN_CONV = 4
GDN_CHUNK = 64
GDN_W = GDN_HEADS * HEAD_DIM
GDN_CONV_W = 3 * GDN_W
MOBA_HEADS = 10
MOBA_BLOCK = 256
MOBA_TOPK = 3
MOBA_W = MOBA_HEADS * HEAD_DIM
DIFF_HEADS = 5
DIFF_W = 2 * DIFF_HEADS * HEAD_DIM
N_EXPERTS = 32
TOP_K = 4
SWIGLU_LIMIT = 7.0
SWIGLU_ALPHA = 1.702
N_MOD = 6
RMS_EPS = 1e-6
L2_EPS = 1e-6
PAGE_SIZE = 128

LANES = 128
GDN_TILE = 256
GDN_HEAD_GROUP = 4
GDN_INTRA_HEADS = 2
ATT_QTILE = 256
DEC_PAGES_PER_STEP = 4
MOE_BM = 512
MOE_GRAN = 128
MOE_W_SPLIT = 2
VMEM_LIMIT_BYTES =56 * 1024 * 1024

Z_GATE_A = GDN_CONV_W
Z_QB = Z_GATE_A + GDN_W
Z_KB = Z_QB + MOBA_W
Z_VB = Z_KB + MOBA_W
Z_QC = Z_VB + MOBA_W
Z_KC = Z_QC + DIFF_W
Z_VC = Z_KC + DIFF_W
Z_BR = Z_VC + DIFF_W


def _cparams(*sem):
    return pltpu.CompilerParams(dimension_semantics=sem, vmem_limit_bytes=VMEM_LIMIT_BYTES)


def _dot(a, b):
    return jnp.dot(a, b, preferred_element_type=f32)


def _dot_nt(a, b, precision=None):
    return lax.dot_general(a, b, (((1,), (1,)), ((), ())), preferred_element_type=f32, precision=precision)


def _dot_tn(a, b):
    return lax.dot_general(a, b, (((0,), (0,)), ((), ())), preferred_element_type=f32)


def _sigmoid(x):
    return 1.0 / (1.0 + jnp.exp(-x))


def _norm_body(x_ref, w_ref, *rest, modulate):
    o_ref = rest[-1]
    x = x_ref[0]
    y = x * lax.rsqrt(jnp.mean(x * x, axis=-1, keepdims=True) + RMS_EPS) * w_ref[...]
    if modulate:
        sc_ref, sh_ref = rest[0], rest[1]
        y = y * (1.0 + sc_ref[0]) + sh_ref[0]
    o_ref[0] = y.astype(o_ref.dtype)


def rms_norm_mod(x, w, sc=None, sh=None, out_dtype=bf16):
    B, L, D = x.shape
    tl = min(L, 512)
    modulate = sc is not None
    in_specs = [pl.BlockSpec((1, tl, D), lambda b, i: (b, i, 0)),
                pl.BlockSpec((1, D), lambda b, i: (0, 0))]
    args = [x, w.reshape(1, D)]
    if modulate:
        in_specs += [pl.BlockSpec((1, 1, D), lambda b, i: (b, 0, 0))] * 2
        args += [sc.reshape(B, 1, D), sh.reshape(B, 1, D)]
    return pl.pallas_call(
        functools.partial(_norm_body, modulate=modulate),
        out_shape=jax.ShapeDtypeStruct((B, L, D), out_dtype),
        grid=(B, L // tl),
        in_specs=in_specs,
        out_specs=pl.BlockSpec((1, tl, D), lambda b, i: (b, i, 0)),
        compiler_params=_cparams("parallel", "parallel"),
        name="rms_norm_mod",
    )(*args)


def _mm_bias_body(a_ref, *rest, k_split):
    w_refs, bias_ref, o_ref = rest[:k_split], rest[k_split], rest[k_split + 1]
    kq = w_refs[0].shape[0]
    acc = bias_ref[...]
    for q, w_ref in enumerate(w_refs):
        acc = acc + _dot(a_ref[:, q * kq:(q + 1) * kq].astype(bf16), w_ref[...].astype(bf16))
    o_ref[...] = acc


def matmul_bias(a, w, layer, bias, *, tn, k_split, name):
    M, K = a.shape
    N = w.shape[-1]
    assert N % tn == 0 and K % k_split == 0
    w_specs = [pl.BlockSpec((None, K // k_split, tn), functools.partial(lambda q, j: (layer, q, j), q))
               for q in range(k_split)]
    return pl.pallas_call(
        functools.partial(_mm_bias_body, k_split=k_split),
        out_shape=jax.ShapeDtypeStruct((M, N), f32),
        grid=(N // tn,),
        in_specs=[pl.BlockSpec((M, K), lambda j: (0, 0))] + w_specs + [pl.BlockSpec((1, tn), lambda j: (0, j))],
        out_specs=pl.BlockSpec((M, tn), lambda j: (0, j)),
        compiler_params=_cparams("parallel"),
        name=name,
    )(a, *([w] * k_split), bias.reshape(1, N))


def _mm_nt_body(a_ref, w_ref, o_ref):
    w = w_ref[0] if len(w_ref.shape) == 3 else w_ref[...]
    o_ref[...] = _dot_nt(a_ref[...].astype(bf16), w.astype(bf16)).astype(o_ref.dtype)


def matmul_nt(a, wt, *, tm, tn, layer=None, out_dtype=f32, name="matmul_nt"):
    M, K = a.shape
    N = wt.shape[-2]
    assert M % tm == 0 and N % tn == 0
    if layer is None:
        w_spec = pl.BlockSpec((tn, K), lambda i, j: (j, 0))
    else:
        w_spec = pl.BlockSpec((1, tn, K), lambda i, j: (layer, j, 0))
    return pl.pallas_call(
        _mm_nt_body,
        out_shape=jax.ShapeDtypeStruct((M, N), out_dtype),
        grid=(M // tm, N // tn),
        in_specs=[pl.BlockSpec((tm, K), lambda i, j: (i, 0)), w_spec],
        out_specs=pl.BlockSpec((tm, tn), lambda i, j: (i, j)),
        compiler_params=_cparams("parallel", "parallel"),
        name=name,
    )(a, wt)


def _repack_body(lo_ref, hi_ref, o_ref):
    half = lo_ref.shape[1]
    o_ref[0, :half, :] = lo_ref[0].astype(o_ref.dtype)
    o_ref[0, half:, :] = hi_ref[0].astype(o_ref.dtype)


def repack_rows_bf16(wt, start, width, tn):
    depth, n_all, K = wt.shape
    N = n_all - width
    sub = 8
    assert N % tn == 0 and start % tn == 0 and width % sub == 0 and tn % (2 * sub) == 0
    first_after = start // tn

    def rows(j, half):
        return (j * (tn // sub) + half * (tn // (2 * sub)) + jnp.where(j >= first_after, width // sub, 0)) * sub

    part = lambda half: pl.BlockSpec((pl.Element(1), pl.Element(tn // 2), pl.Element(K)),
                                     lambda l, j: (l, rows(j, half), 0))
    return pl.pallas_call(
        _repack_body,
        out_shape=jax.ShapeDtypeStruct((depth, N, K), bf16),
        grid=(depth, N // tn),
        in_specs=[part(0), part(1)],
        out_specs=pl.BlockSpec((1, tn, K), lambda l, j: (l, j, 0)),
        compiler_params=_cparams("parallel", "parallel"),
        name="repack_rows_bf16",
    )(wt, wt)


def _softplus(x):
    return jnp.maximum(x, 0.0) + jnp.log1p(jnp.exp(-jnp.abs(x)))


def _gdn_prep_body(x_ref, prev_ref, buf_ref, cw_ref, zs_ref, al_ref, dtb_ref,
                   q_ref, k_ref, v_ref, g_ref, beta_ref, *, tl, valid_len):
    i = pl.program_id(1)
    row = lax.broadcasted_iota(jnp.int32, (tl, 1), 0)
    valid = (i * tl + row) < valid_len
    row8 = lax.broadcasted_iota(jnp.int32, (8, LANES), 0)
    first = i == 0
    outs = (q_ref, k_ref, v_ref)
    for part in range(3):
        for h in range(GDN_HEADS):
            c0 = part * GDN_W + h * HEAD_DIM
            cols = slice(c0, c0 + HEAD_DIM)
            x = x_ref[0, :, cols]
            prev = jnp.where(first, buf_ref[0, :, cols], prev_ref[0, :, cols])
            y = x * cw_ref[GDN_CONV - 1:GDN_CONV, cols]
            for s in range(1, GDN_CONV):
                xs = pltpu.roll(x, s, axis=0)
                top = jnp.where(row8 < s, pltpu.roll(prev, s, axis=0), xs[:8])
                xs = jnp.concatenate([top, xs[8:]], axis=0)
                y = y + xs * cw_ref[GDN_CONV - 1 - s:GDN_CONV - s, cols]
            y = y * _sigmoid(y)
            if part < 2:
                y = y * lax.rsqrt(jnp.sum(y * y, axis=-1, keepdims=True) + L2_EPS)
            if part == 0:
                y = y * (HEAD_DIM ** -0.5)
            outs[part][0, :, h * HEAD_DIM:(h + 1) * HEAD_DIM] = jnp.where(valid, y, 0.0)
    zs = zs_ref[0]
    g = -jnp.exp(al_ref[...]) * _softplus(zs[:, :LANES] + dtb_ref[...])
    g = jnp.where(valid, g, 0.0)
    beta_ref[0] = jnp.where(valid, _sigmoid(zs[:, LANES:]), 0.0)
    in_chunk = row % GDN_CHUNK
    s = 1
    while s < GDN_CHUNK:
        g = g + jnp.where(in_chunk >= s, pltpu.roll(g, s, axis=0), 0.0)
        s *= 2
    g_ref[0] = g


def gdn_prep(zqkv, zs, buf8, conv_w, a_log, dt_bias, valid_len, tl):
    B, L, _ = zqkv.shape
    assert L % tl == 0 and tl % GDN_CHUNK == 0
    nprev = tl // 8
    pad = lambda t: jnp.pad(t.astype(f32), (0, LANES - GDN_HEADS)).reshape(1, LANES)
    row_spec = lambda w: pl.BlockSpec((1, tl, w), lambda b, i: (b, i, 0))
    outs = pl.pallas_call(
        functools.partial(_gdn_prep_body, tl=tl, valid_len=valid_len),
        out_shape=[jax.ShapeDtypeStruct((B, L, GDN_W), f32)] * 3 + [jax.ShapeDtypeStruct((B, L, LANES), f32)] * 2,
        grid=(B, L // tl),
        in_specs=[row_spec(GDN_CONV_W),
                  pl.BlockSpec((1, 8, GDN_CONV_W), lambda b, i: (b, jnp.maximum(i * nprev - 1, 0), 0)),
                  pl.BlockSpec((1, 8, GDN_CONV_W), lambda b, i: (b, 0, 0)),
                  pl.BlockSpec((GDN_CONV, GDN_CONV_W), lambda b, i: (0, 0)),
                  row_spec(2 * LANES),
                  pl.BlockSpec((1, LANES), lambda b, i: (0, 0)),
                  pl.BlockSpec((1, LANES), lambda b, i: (0, 0))],
        out_specs=[row_spec(GDN_W)] * 3 + [row_spec(LANES)] * 2,
        compiler_params=_cparams("parallel", "parallel"),
        name="gdn_prep",
    )(zqkv, zqkv, buf8, conv_w, zs, pad(a_log), pad(dt_bias))
    return outs


def _split_bf16(a):
    hi = a.astype(bf16)
    return hi, (a - hi.astype(f32)).astype(bf16)


def _dot_split(a, b):
    return _dot(a[0], b[0]) + _dot(a[0], b[1]) + _dot(a[1], b[0])


def _gdn_intra_body(q_ref, k_ref, v_ref, g_ref, beta_ref,
                    w1_ref, w2_ref, qk_ref, qg_ref, kd_ref, *, tl, nil):
    nh = GDN_INTRA_HEADS
    hg = pl.program_id(1)
    lane = lax.broadcasted_iota(jnp.int32, (tl, LANES), 1)
    r = lax.broadcasted_iota(jnp.int32, (tl, tl), 0)
    c = lax.broadcasted_iota(jnp.int32, (tl, tl), 1)
    same = (r // GDN_CHUNK) == (c // GDN_CHUNK)
    eye = r == c
    is_last = c == (r // GDN_CHUNK) * GDN_CHUNK + (GDN_CHUNK - 1)
    heads = []
    for hh in range(nh):
        cols = slice(hh * HEAD_DIM, (hh + 1) * HEAD_DIM)
        h = hg * nh + hh
        gcol = jnp.sum(jnp.where(lane == h, g_ref[0], 0.0), axis=-1, keepdims=True)
        bcol = jnp.sum(jnp.where(lane == h, beta_ref[0], 0.0), axis=-1, keepdims=True)
        grow = jnp.sum(jnp.where(eye, gcol, 0.0), axis=0, keepdims=True)
        decay = jnp.exp(jnp.where(same & (r >= c), gcol - grow, -jnp.inf))
        k = k_ref[0, :, cols]
        q = q_ref[0, :, cols]
        kb = k.astype(bf16)
        n = jnp.where(same & (r > c), -(bcol * _dot_nt(kb, kb) * decay), 0.0)
        gam = jnp.exp(gcol)
        qk_ref[0, hh] = _dot_nt(q.astype(bf16), kb) * decay
        qg_ref[0, :, cols] = q * gam
        glast = jnp.sum(jnp.where(is_last, grow, 0.0), axis=-1, keepdims=True)
        kd_ref[0, :, cols] = k * jnp.exp(glast - gcol)
        rhs = jnp.concatenate([bcol * v_ref[0, :, cols], (bcol * gam) * k], axis=1)
        heads.append(dict(t=jnp.where(eye, 1.0, 0.0) + n, p=_split_bf16(n), rhs=_split_bf16(rhs)))
    span = 2
    while span < nil:
        for st in heads:
            st["p"] = _split_bf16(_dot_split(st["p"], st["p"]))
        for st in heads:
            st["t"] = st["t"] + _dot_split(_split_bf16(st["t"]), st["p"])
        span *= 2
    ws = [_dot_split(_split_bf16(st["t"]), st["rhs"]) for st in heads]
    w1_ref[0] = jnp.concatenate([w[:, :HEAD_DIM] for w in ws], axis=1)
    w2_ref[0] = jnp.concatenate([w[:, HEAD_DIM:] for w in ws], axis=1)


def gdn_intra(q, k, v, g, beta, tl, nil):
    B, L, _ = q.shape
    nh = GDN_INTRA_HEADS
    head_spec = pl.BlockSpec((1, tl, nh * HEAD_DIM), lambda b, h, i: (b, i, h))
    all_spec = pl.BlockSpec((1, tl, LANES), lambda b, h, i: (b, i, 0))
    wide = jax.ShapeDtypeStruct((B, L, GDN_W), f32)
    return pl.pallas_call(
        functools.partial(_gdn_intra_body, tl=tl, nil=nil),
        out_shape=[wide, wide, jax.ShapeDtypeStruct((B, GDN_HEADS, L, tl), f32), wide, wide],
        grid=(B, GDN_HEADS // nh, L // tl),
        in_specs=[head_spec, head_spec, head_spec, all_spec, all_spec],
        out_specs=[head_spec, head_spec,
                   pl.BlockSpec((1, nh, tl, tl), lambda b, h, i: (b, h, i, 0)),
                   head_spec, head_spec],
        compiler_params=_cparams("parallel", "parallel", "parallel"),
        name="gdn_intra",
    )(q, k, v, g, beta)


def _gdn_seq_body(w1_ref, w2_ref, qg_ref, kd_ref, qk_ref, g_ref, gate_ref, s0_ref, nw_ref,
                  o_ref, so_ref, s_scr, u_scr, *, tl):
    hg = pl.program_id(1)
    ti = pl.program_id(2)

    @pl.when(ti == 0)
    def _():
        s_scr[...] = s0_ref[0]

    u_scr[...] = jnp.zeros_like(u_scr)
    lane = lax.broadcasted_iota(jnp.int32, (1, LANES), 1)
    for ci in range(tl // GDN_CHUNK):
        rows = slice(ci * GDN_CHUNK, (ci + 1) * GDN_CHUNK)
        last = ci * GDN_CHUNK + GDN_CHUNK - 1
        g_last_row = g_ref[0, last:last + 1, :]
        for hh in range(GDN_HEAD_GROUP):
            cols = slice(hh * HEAD_DIM, (hh + 1) * HEAD_DIM)
            s = s_scr[hh]
            sb = s.astype(bf16)
            u = w1_ref[0, rows, cols] - _dot(w2_ref[0, rows, cols].astype(bf16), sb)
            u_scr[hh, rows, :] = u
            o = _dot(qg_ref[0, rows, cols].astype(bf16), sb) + _dot(qk_ref[0, hh, rows, :].astype(bf16),
                                                                  u_scr[hh].astype(bf16))
            g_last = jnp.sum(jnp.where(lane == hg * GDN_HEAD_GROUP + hh, g_last_row, 0.0), axis=-1, keepdims=True)
            s_scr[hh] = jnp.exp(g_last) * s + _dot_tn(kd_ref[0, rows, cols].astype(bf16), u.astype(bf16))
            y = o * lax.rsqrt(jnp.mean(o * o, axis=-1, keepdims=True) + RMS_EPS) * nw_ref[...]
            gate = gate_ref[0, rows, cols]
            o_ref[0, rows, cols] = (y * (gate * _sigmoid(gate))).astype(o_ref.dtype)

    @pl.when(ti == pl.num_programs(2) - 1)
    def _():
        so_ref[0] = s_scr[...]


def gdn_seq(w1, w2, qg, kd, qk, g, gate_arr, gate_col0, state0, norm_w, tl):
    B, L, _ = w1.shape
    gw = GDN_HEAD_GROUP * HEAD_DIM
    assert gate_col0 % gw == 0
    gcb = gate_col0 // gw
    grp = pl.BlockSpec((1, tl, gw), lambda b, hg, i: (b, i, hg))
    return pl.pallas_call(
        functools.partial(_gdn_seq_body, tl=tl),
        out_shape=[jax.ShapeDtypeStruct((B, L, GDN_W), bf16),
                   jax.ShapeDtypeStruct((B, GDN_HEADS, HEAD_DIM, HEAD_DIM), f32)],
        grid=(B, GDN_HEADS // GDN_HEAD_GROUP, L // tl),
        in_specs=[grp, grp, grp, grp,
                  pl.BlockSpec((1, GDN_HEAD_GROUP, tl, tl), lambda b, hg, i: (b, hg, i, 0)),
                  pl.BlockSpec((1, tl, LANES), lambda b, hg, i: (b, i, 0)),
                  pl.BlockSpec((1, tl, gw), lambda b, hg, i: (b, i, gcb + hg)),
                  pl.BlockSpec((1, GDN_HEAD_GROUP, HEAD_DIM, HEAD_DIM), lambda b, hg, i: (b, hg, 0, 0)),
                  pl.BlockSpec((1, HEAD_DIM), lambda b, hg, i: (0, 0))],
        out_specs=[grp,
                   pl.BlockSpec((1, GDN_HEAD_GROUP, HEAD_DIM, HEAD_DIM), lambda b, hg, i: (b, hg, 0, 0))],
        scratch_shapes=[pltpu.VMEM((GDN_HEAD_GROUP, HEAD_DIM, HEAD_DIM), f32),
                        pltpu.VMEM((GDN_HEAD_GROUP, tl, HEAD_DIM), f32)],
        compiler_params=_cparams("parallel", "parallel", "arbitrary"),
        name="gdn_seq",
    )(w1, w2, qg, kd, qk, g, gate_arr, state0, norm_w.reshape(1, HEAD_DIM))


def _topk_mask(score, lane_or_row, n_valid, axis, size):
    sel = jnp.zeros(score.shape, jnp.bool_)
    for r in range(MOBA_TOPK):
        m = jnp.max(score, axis=axis, keepdims=True)
        idx = jnp.min(jnp.where(score == m, lane_or_row, size), axis=axis, keepdims=True)
        hit = lane_or_row == idx
        sel = sel | (hit & (r < n_valid))
        score = jnp.where(hit, -jnp.inf, score)
    return sel


def _moba_prompt_body(q_ref, k_ref, v_ref, o_ref, ko_ref, vo_ref, kmean_scr, *, seq):
    nblk = seq // MOBA_BLOCK

    @pl.when(pl.program_id(2) == 0)
    def _():
        k = k_ref[0]
        ko_ref[0, 0] = k
        vo_ref[0, 0] = v_ref[0]
        kmean = jnp.mean(k.reshape(nblk, MOBA_BLOCK, HEAD_DIM), axis=1)
        kmean_scr[...] = jnp.concatenate([kmean, jnp.zeros((LANES - nblk, HEAD_DIM), f32)], axis=0)

    for n in range(nblk):
        pl.when(pl.program_id(2) == n)(
            functools.partial(_moba_prompt_tile, q_ref, k_ref, v_ref, o_ref, kmean_scr, n))


def _moba_prompt_tile(q_ref, k_ref, v_ref, o_ref, kmean_scr, i):
    tq = ATT_QTILE
    seq = (i + 1) * MOBA_BLOCK
    q = q_ref[0]
    k = k_ref[0, :seq, :]
    lane = lax.broadcasted_iota(jnp.int32, (tq, LANES), 1)
    score = _dot_nt(q, kmean_scr[...], precision=lax.Precision.HIGHEST)
    score = jnp.where(lane < i, score, -jnp.inf)
    sel = _topk_mask(score, lane, i, 1, LANES)
    key = lax.broadcasted_iota(jnp.int32, (LANES, seq), 1)
    blk_row = lax.broadcasted_iota(jnp.int32, (LANES, seq), 0)
    member = jnp.where(key // MOBA_BLOCK == blk_row, 1.0, 0.0).astype(bf16)
    sel_keys = _dot(jnp.where(sel, 1.0, 0.0).astype(bf16), member) > 0.5
    kpos = lax.broadcasted_iota(jnp.int32, (tq, seq), 1)
    qpos = i * tq + lax.broadcasted_iota(jnp.int32, (tq, seq), 0)
    own = (kpos // MOBA_BLOCK == i) & (kpos <= qpos)
    s = _dot_nt(q.astype(bf16), k.astype(bf16)) * (HEAD_DIM ** -0.5)
    s = jnp.where(sel_keys | own, s, -jnp.inf)
    p = jnp.exp(s - jnp.max(s, axis=-1, keepdims=True))
    p = p / jnp.sum(p, axis=-1, keepdims=True)
    o_ref[0] = _dot(p.astype(bf16), v_ref[0, :seq, :].astype(bf16)).astype(o_ref.dtype)


def moba_prompt(z, B, L):
    tq = ATT_QTILE
    assert L % tq == 0 and tq == MOBA_BLOCK
    qb, kb, vb = Z_QB // HEAD_DIM, Z_KB // HEAD_DIM, Z_VB // HEAD_DIM
    rows = jax.ShapeDtypeStruct((B, MOBA_HEADS, L, HEAD_DIM), f32)
    rows_spec = pl.BlockSpec((1, 1, L, HEAD_DIM), lambda b, h, i: (b, h, 0, 0))
    return pl.pallas_call(
        functools.partial(_moba_prompt_body, seq=L),
        out_shape=[jax.ShapeDtypeStruct((B, L, MOBA_W), bf16), rows, rows],
        grid=(B, MOBA_HEADS, L // tq),
        in_specs=[pl.BlockSpec((1, tq, HEAD_DIM), lambda b, h, i: (b, i, qb + h)),
                  pl.BlockSpec((1, L, HEAD_DIM), lambda b, h, i: (b, 0, kb + h)),
                  pl.BlockSpec((1, L, HEAD_DIM), lambda b, h, i: (b, 0, vb + h))],
        out_specs=[pl.BlockSpec((1, tq, HEAD_DIM), lambda b, h, i: (b, i, h)), rows_spec, rows_spec],
        scratch_shapes=[pltpu.VMEM((LANES, HEAD_DIM), f32)],
        compiler_params=_cparams("parallel", "parallel", "arbitrary"),
        name="moba_prompt",
    )(z, z, z)


def _diff_lambda(dl_ref, lam_init):
    dl = dl_ref[...]
    a = jnp.sum(dl[0:1] * dl[1:2], axis=-1, keepdims=True)
    b = jnp.sum(dl[2:3] * dl[3:4], axis=-1, keepdims=True)
    return jnp.exp(a) - jnp.exp(b) + lam_init


def _diff_prompt_body(q_ref, k_ref, v_ref, dl_ref, nw_ref, o_ref, ko_ref, vo_ref, *, seq, lam_init):
    @pl.when(pl.program_id(2) == 0)
    def _():
        ko_ref[0, 0] = k_ref[0, :, :HEAD_DIM]
        ko_ref[0, 1] = k_ref[0, :, HEAD_DIM:]
        vo_ref[0, 0] = v_ref[0]

    for n in range(seq // ATT_QTILE):
        pl.when(pl.program_id(2) == n)(
            functools.partial(_diff_prompt_tile, q_ref, k_ref, v_ref, dl_ref, nw_ref, o_ref, n, lam_init))


def _diff_prompt_tile(q_ref, k_ref, v_ref, dl_ref, nw_ref, o_ref, i, lam_init):
    tq = ATT_QTILE
    seq = (i + 1) * tq
    lam = _diff_lambda(dl_ref, lam_init)
    kpos = lax.broadcasted_iota(jnp.int32, (tq, seq), 1)
    qpos = i * tq + lax.broadcasted_iota(jnp.int32, (tq, seq), 0)
    causal = kpos <= qpos
    ps = []
    for m in range(2):
        cols = slice(m * HEAD_DIM, (m + 1) * HEAD_DIM)
        s = _dot_nt(q_ref[0, :, cols].astype(bf16), k_ref[0, :seq, cols].astype(bf16)) * (HEAD_DIM ** -0.5)
        s = jnp.where(causal, s, -jnp.inf)
        p = jnp.exp(s - jnp.max(s, axis=-1, keepdims=True))
        ps.append(p / jnp.sum(p, axis=-1, keepdims=True))
    w = ps[0] - lam * ps[1]
    o = _dot(w.astype(bf16), v_ref[0, :seq, :].astype(bf16))
    y = o * lax.rsqrt(jnp.mean(o * o, axis=-1, keepdims=True) + RMS_EPS) * nw_ref[...]
    o_ref[0] = (y * (1.0 - lam_init)).astype(o_ref.dtype)


def diff_prompt(z, B, L, diff_lambda, norm_w, lam_init):
    tq = ATT_QTILE
    hw = 2 * HEAD_DIM
    qb, kb, vb = Z_QC // hw, Z_KC // hw, Z_VC // hw
    return pl.pallas_call(
        functools.partial(_diff_prompt_body, seq=L, lam_init=lam_init),
        out_shape=[jax.ShapeDtypeStruct((B, L, DIFF_W), bf16),
                   jax.ShapeDtypeStruct((B, 2 * DIFF_HEADS, L, HEAD_DIM), f32),
                   jax.ShapeDtypeStruct((B, DIFF_HEADS, L, hw), f32)],
        grid=(B, DIFF_HEADS, L // tq),
        in_specs=[pl.BlockSpec((1, tq, hw), lambda b, h, i: (b, i, qb + h)),
                  pl.BlockSpec((1, L, hw), lambda b, h, i: (b, 0, kb + h)),
                  pl.BlockSpec((1, L, hw), lambda b, h, i: (b, 0, vb + h)),
                  pl.BlockSpec((4, HEAD_DIM), lambda b, h, i: (0, 0)),
                  pl.BlockSpec((1, hw), lambda b, h, i: (0, 0))],
        out_specs=[pl.BlockSpec((1, tq, hw), lambda b, h, i: (b, i, h)),
                   pl.BlockSpec((1, 2, L, HEAD_DIM), lambda b, h, i: (b, h, 0, 0)),
                   pl.BlockSpec((1, 1, L, hw), lambda b, h, i: (b, h, 0, 0))],
        compiler_params=_cparams("parallel", "parallel", "arbitrary"),
        name="diff_prompt",
    )(z, z, z, diff_lambda, norm_w.reshape(1, hw))


def _decode_attn_body(pt_ref, qx_ref, knew_ref, vnew_ref, dl_ref, nw_ref, *rest,
                      mode, n_steps, n_new, lam_init):
    pp = DEC_PAGES_PER_STEP
    k_refs = rest[:pp]
    v_refs = rest[pp:2 * pp]
    o_ref = rest[2 * pp]
    s_scr, bsum_scr, bmax_scr, sel_scr, inv_scr, pnew_scr, acc_scr = rest[2 * pp + 1:]
    ph = pl.program_id(1)
    g = pl.program_id(2)
    rows_step = pp * PAGE_SIZE
    blk_step = rows_step // MOBA_BLOCK
    nblk = n_steps * blk_step
    n_kh = MOBA_HEADS
    hv = acc_scr.shape[0]
    kh_per_v = n_kh // hv
    scale = HEAD_DIM ** -0.5

    def scores(load_head):
        acc = None
        for hp in range(n_kh // 2):
            lhs = jnp.concatenate([load_head(2 * hp), load_head(2 * hp + 1)], axis=1).astype(bf16)
            d = _dot_nt(lhs, qx_ref[0, hp])
            acc = d if acc is None else acc + d
        return acc

    @pl.when(ph == 0)
    def _():
        s = scores(lambda h: jnp.concatenate([k_refs[j][h] for j in range(pp)], axis=0))
        s_scr[pl.ds(pl.multiple_of(g * rows_step, rows_step), rows_step), :] = s
        for j in range(blk_step):
            blk = s[j * MOBA_BLOCK:(j + 1) * MOBA_BLOCK]
            bsum_scr[pl.ds(g * blk_step + j, 1), :] = jnp.mean(blk, axis=0, keepdims=True)
            bmax_scr[pl.ds(g * blk_step + j, 1), :] = jnp.max(blk, axis=0, keepdims=True)

    @pl.when((ph == 1) & (g == 0))
    def _():
        col = lax.broadcasted_iota(jnp.int32, (LANES, LANES), 1)
        krow = lax.broadcasted_iota(jnp.int32, (LANES, LANES), 0)
        new_ok = (krow < n_new) & (krow <= col % n_new)
        s_new = jnp.where(new_ok, scores(lambda h: knew_ref[0, h]) * scale, -jnp.inf)
        bmax = bmax_scr[...]
        if mode == "moba":
            brow = lax.broadcasted_iota(jnp.int32, (nblk, LANES), 0)
            sel = _topk_mask(bsum_scr[...], brow, nblk, 0, nblk)
            sel_scr[...] = jnp.where(sel, 1.0, 0.0)
            bmax = jnp.where(sel, bmax, -jnp.inf)
        m = jnp.maximum(jnp.max(bmax, axis=0, keepdims=True) * scale, jnp.max(s_new, axis=0, keepdims=True))

        def exp_block(b, tot):
            r0 = pl.multiple_of(b * MOBA_BLOCK, MOBA_BLOCK)
            s = s_scr[pl.ds(r0, MOBA_BLOCK), :] * scale
            if mode == "moba":
                s = jnp.where(sel_scr[pl.ds(b, 1), :] > 0.5, s, -jnp.inf)
            p = jnp.exp(s - m)
            s_scr[pl.ds(r0, MOBA_BLOCK), :] = p
            return tot + jnp.sum(p, axis=0, keepdims=True)

        p_new = jnp.exp(s_new - m)
        tot = lax.fori_loop(0, nblk, exp_block, jnp.sum(p_new, axis=0, keepdims=True))
        inv = 1.0 / tot
        inv_scr[...] = jnp.broadcast_to(inv, inv_scr.shape)
        pnew_scr[...] = p_new * inv
        acc_scr[...] = jnp.zeros_like(acc_scr)

    def accumulate(p, load_vhead):
        pt = p.T.astype(bf16)
        for h in range(hv):
            acc_scr[h] += _dot(pt, load_vhead(h).astype(bf16))

    @pl.when(ph == 1)
    def _():
        p = s_scr[pl.ds(pl.multiple_of(g * rows_step, rows_step), rows_step), :] * inv_scr[0:1, :]
        accumulate(p, lambda h: jnp.concatenate([v_refs[j][h] for j in range(pp)], axis=0))

    @pl.when((ph == 1) & (g == n_steps - 1))
    def _():
        accumulate(pnew_scr[...], lambda h: vnew_ref[0, h])
        dv = acc_scr.shape[2]
        rows_h = kh_per_v * n_new
        if mode == "diff":
            lam = _diff_lambda(dl_ref, lam_init)
        for h in range(hv):
            a = acc_scr[h, h * rows_h:(h + 1) * rows_h, :]
            if mode == "diff":
                o = a[:n_new] - lam * a[n_new:]
                y = o * lax.rsqrt(jnp.mean(o * o, axis=-1, keepdims=True) + RMS_EPS) * nw_ref[...]
                a = y * (1.0 - lam_init)
            o_ref[0, :, h * dv:(h + 1) * dv] = a.astype(o_ref.dtype)


def decode_attention(mode, layer, page_table, q, k_new, v_new, cache_k, cache_v,
                     diff_lambda, norm_w, lam_init):
    B, n_new, n_kh, _ = q.shape
    hv, dv = v_new.shape[2], v_new.shape[3]
    n_pages = page_table.shape[1]
    pp = DEC_PAGES_PER_STEP
    assert n_pages % pp == 0 and n_new == 8 and n_kh == MOBA_HEADS
    n_steps = n_pages // pp
    n_keys = n_pages * PAGE_SIZE
    eye = jnp.eye(n_kh, dtype=f32)
    qx = jnp.einsum("bqhd,hg->bhgqd", q, eye).reshape(B, n_kh, n_kh * n_new, HEAD_DIM)
    qx = jnp.pad(qx, ((0, 0), (0, 0), (0, LANES - n_kh * n_new), (0, 0)))
    qx = qx.reshape(B, n_kh // 2, 2, LANES, HEAD_DIM).transpose(0, 1, 3, 2, 4).reshape(B, n_kh // 2, LANES, 2 * HEAD_DIM)
    qx = qx.astype(bf16)
    pad_rows = lambda t: jnp.pad(t.transpose(0, 2, 1, 3), ((0, 0), (0, 0), (0, PAGE_SIZE - n_new), (0, 0)))
    knew, vnew = pad_rows(k_new), pad_rows(v_new)

    def k_map(j):
        return lambda b, ph, g, pt: (layer, pt[b, (g * (1 - ph) + (n_steps - 1) * ph) * pp + j], 0, 0, 0)

    def v_map(j):
        return lambda b, ph, g, pt: (layer, pt[b, g * ph * pp + j], 0, 0, 0)

    per_b = lambda *shape: pl.BlockSpec((1,) + shape, lambda b, ph, g, pt: (b,) + (0,) * len(shape))
    grid_spec = pltpu.PrefetchScalarGridSpec(
        num_scalar_prefetch=1,
        grid=(B, 2, n_steps),
        in_specs=[per_b(n_kh // 2, LANES, 2 * HEAD_DIM),
                  per_b(n_kh, PAGE_SIZE, HEAD_DIM),
                  per_b(hv, PAGE_SIZE, dv),
                  pl.BlockSpec((4, HEAD_DIM), lambda b, ph, g, pt: (0, 0)),
                  pl.BlockSpec((1, dv), lambda b, ph, g, pt: (0, 0))]
                 + [pl.BlockSpec((None, None, n_kh, PAGE_SIZE, HEAD_DIM), k_map(j)) for j in range(pp)]
                 + [pl.BlockSpec((None, None, hv, PAGE_SIZE, dv), v_map(j)) for j in range(pp)],
        out_specs=per_b(n_new, hv * dv),
        scratch_shapes=[pltpu.VMEM((n_keys, LANES), f32),
                        pltpu.VMEM((n_keys // MOBA_BLOCK, LANES), f32),
                        pltpu.VMEM((n_keys // MOBA_BLOCK, LANES), f32),
                        pltpu.VMEM((n_keys // MOBA_BLOCK, LANES), f32),
                        pltpu.VMEM((8, LANES), f32),
                        pltpu.VMEM((PAGE_SIZE, LANES), f32),
                        pltpu.VMEM((hv, LANES, dv), f32)],
    )
    return pl.pallas_call(
        functools.partial(_decode_attn_body, mode=mode, n_steps=n_steps, n_new=n_new, lam_init=lam_init),
        out_shape=jax.ShapeDtypeStruct((B, n_new, hv * dv), f32),
        grid_spec=grid_spec,
        compiler_params=_cparams("arbitrary", "arbitrary", "arbitrary"),
        name="decode_attn_" + mode,
    )(page_table, qx, knew, vnew, diff_lambda, norm_w.reshape(1, dv),
      *([cache_k] * pp), *([cache_v] * pp))


def _branch_body(oa_ref, ob_ref, oc_ref, wa_ref, wb_ref, wc_ref, ba_ref, bb_ref, bc_ref, o_ref):
    m = (_sigmoid(ba_ref[...]) * _dot(oa_ref[...], wa_ref[...])
         + _sigmoid(bb_ref[...]) * _dot(ob_ref[...], wb_ref[...])
         + _sigmoid(bc_ref[...]) * _dot(oc_ref[...], wc_ref[...]))
    o_ref[...] = m.astype(o_ref.dtype)


def branch_merge(oa, ob, oc, wa, wb, wc, z, tm):
    M = oa.shape[0]
    D = wa.shape[1]
    tn = 512
    assert Z_BR % tn == 0 and D % tn == 0 and M % tm == 0
    br0 = Z_BR // tn
    nd = D // tn
    act = lambda w: pl.BlockSpec((tm, w), lambda i, j: (i, 0))
    wsp = lambda kdim: pl.BlockSpec((kdim, tn), lambda i, j: (0, j))
    brs = lambda n: pl.BlockSpec((tm, tn), lambda i, j: (i, br0 + n * nd + j))
    return pl.pallas_call(
        _branch_body,
        out_shape=jax.ShapeDtypeStruct((M, D), bf16),
        grid=(M // tm, nd),
        in_specs=[act(oa.shape[1]), act(ob.shape[1]), act(oc.shape[1]),
                  wsp(wa.shape[0]), wsp(wb.shape[0]), wsp(wc.shape[0]),
                  brs(0), brs(1), brs(2)],
        out_specs=pl.BlockSpec((tm, tn), lambda i, j: (i, j)),
        compiler_params=_cparams("parallel", "parallel"),
        name="branch_merge",
    )(oa, ob, oc, wa, wb, wc, z, z, z)


def _proj_resid_body(a_ref, w_ref, x_ref, g_ref, o_ref):
    g = g_ref[0] if len(g_ref.shape) == 3 else g_ref[...]
    o_ref[...] = x_ref[...] + g * _dot(a_ref[...], w_ref[...])


def proj_residual(a, w, x, gate, rows_per_seq, tm):
    M, K = a.shape
    D = w.shape[1]
    tn = 512
    if rows_per_seq % tm == 0:
        per = rows_per_seq // tm
        g_arg = gate.reshape(gate.shape[0], 1, D)
        g_spec = pl.BlockSpec((1, 1, tn), lambda i, j: (i // per, 0, j))
    else:
        g_arg = jnp.repeat(gate, rows_per_seq, axis=0)
        g_spec = pl.BlockSpec((tm, tn), lambda i, j: (i, j))
    return pl.pallas_call(
        _proj_resid_body,
        out_shape=jax.ShapeDtypeStruct((M, D), f32),
        grid=(M // tm, D // tn),
        in_specs=[pl.BlockSpec((tm, K), lambda i, j: (i, 0)),
                  pl.BlockSpec((K, tn), lambda i, j: (0, j)),
                  pl.BlockSpec((tm, tn), lambda i, j: (i, j)),
                  g_spec],
        out_specs=pl.BlockSpec((tm, tn), lambda i, j: (i, j)),
        compiler_params=_cparams("parallel", "parallel"),
        name="proj_residual",
    )(a, w, x, g_arg)


def _router_body(h_ref, w_ref, b_ref, idx_ref, gate_ref):
    logits = _dot_nt(h_ref[...], w_ref[...].astype(bf16)) + b_ref[...]
    lane = lax.broadcasted_iota(jnp.int32, logits.shape, 1)
    logits = jnp.where(lane < N_EXPERTS, logits, -jnp.inf)
    idx_out = jnp.zeros(logits.shape, jnp.int32)
    val_out = jnp.full(logits.shape, -jnp.inf, f32)
    for r in range(TOP_K):
        m = jnp.max(logits, axis=-1, keepdims=True)
        idx = jnp.min(jnp.where(logits == m, lane, LANES), axis=-1, keepdims=True)
        idx_out = jnp.where(lane == r, idx, idx_out)
        val_out = jnp.where(lane == r, m, val_out)
        logits = jnp.where(lane == idx, -jnp.inf, logits)
    e = jnp.exp(val_out - jnp.max(val_out, axis=-1, keepdims=True))
    idx_ref[...] = idx_out
    gate_ref[...] = e / jnp.sum(e, axis=-1, keepdims=True)


def router(h, w_router, b_router, tm):
    T, D = h.shape
    w = jnp.pad(w_router.T, ((0, LANES - N_EXPERTS), (0, 0)))
    b = jnp.pad(b_router, (0, LANES - N_EXPERTS)).reshape(1, LANES)
    return pl.pallas_call(
        _router_body,
        out_shape=[jax.ShapeDtypeStruct((T, LANES), jnp.int32), jax.ShapeDtypeStruct((T, LANES), f32)],
        grid=(T // tm,),
        in_specs=[pl.BlockSpec((tm, D), lambda i: (i, 0)),
                  pl.BlockSpec((LANES, D), lambda i: (0, 0)),
                  pl.BlockSpec((1, LANES), lambda i: (0, 0))],
        out_specs=[pl.BlockSpec((tm, LANES), lambda i: (i, 0))] * 2,
        compiler_params=_cparams("parallel"),
        name="router",
    )(h, w, b)


def _cast_rows(src_ref, dst_ref):
    step = 512
    for r0 in range(0, src_ref.shape[0], step):
        dst_ref[r0:r0 + step, :] = src_ref[r0:r0 + step, :].astype(bf16)


def _cast_parts(part_refs, dst_ref):
    rows = part_refs[0].shape[0]
    for q, ref in enumerate(part_refs):
        _cast_rows(ref, dst_ref.at[q * rows:(q + 1) * rows, :])


def _moe_up_body(ie, ij, ib, ifirst, ivalid, x_ref, *rest):
    ns = MOE_W_SPLIT
    wg_refs, wu_refs = rest[:ns], rest[ns:2 * ns]
    bg_ref, bu_ref, o_ref, wg_s, wu_s = rest[2 * ns:]
    i = pl.program_id(0)

    @pl.when(ifirst[i] == 1)
    def _():
        _cast_parts(wg_refs, wg_s)
        _cast_parts(wu_refs, wu_s)

    def act(x):
        gate = jnp.minimum(_dot(x, wg_s[...]) + bg_ref[...], SWIGLU_LIMIT)
        up = jnp.clip(_dot(x, wu_s[...]) + bu_ref[...], -SWIGLU_LIMIT, SWIGLU_LIMIT)
        return ((up + 1.0) * gate * _sigmoid(SWIGLU_ALPHA * gate)).astype(o_ref.dtype)

    _moe_block_cases(ivalid[i], o_ref, lambda rows: act(x_ref[rows, :]))


def _moe_block_cases(kind, o_ref, compute):
    bm, tn = o_ref.shape
    for q in range(1, bm // MOE_GRAN + 1):
        used = q * MOE_GRAN

        @pl.when(kind == q)
        def _(used=used):
            o_ref[:used, :] = compute(slice(0, used))
            if used < bm:
                o_ref[used:, :] = jnp.zeros((bm - used, tn), o_ref.dtype)

    @pl.when(kind == 0)
    def _():
        o_ref[...] = jnp.zeros_like(o_ref)


def _moe_down_body(ie, ij, ib, ifirst, ivalid, a_ref, *rest):
    ns = MOE_W_SPLIT
    w_refs = rest[:ns]
    b_ref, o_ref, w_s = rest[ns:]
    i = pl.program_id(0)

    @pl.when(ifirst[i] == 1)
    def _():
        _cast_parts(w_refs, w_s)

    _moe_block_cases(ivalid[i], o_ref, lambda rows: _dot(a_ref[rows, :], w_s[...]) + b_ref[...])


def _moe_items(nb_e, blk_start, counts, bm, n_blocks_max, n_j):
    n_items = n_blocks_max * n_j
    total = jnp.sum(nb_e) * n_j
    item_end = jnp.cumsum(nb_e) * n_j
    i = jnp.arange(n_items, dtype=jnp.int32)
    i_c = jnp.minimum(i, total - 1)
    e = jnp.minimum(jnp.searchsorted(item_end, i_c, side="right"), N_EXPERTS - 1).astype(jnp.int32)
    r = i_c - blk_start[e] * n_j
    nb = jnp.maximum(nb_e[e], 1)
    valid = (i < total).astype(jnp.int32)
    j = (r // nb).astype(jnp.int32)
    b = jnp.where(valid == 1, blk_start[e] + r % nb, n_blocks_max).astype(jnp.int32)
    first = ((r % nb == 0) & (valid == 1)).astype(jnp.int32)
    rows_used = counts[e] - (r % nb) * bm
    quarters = jnp.clip((rows_used + MOE_GRAN - 1) // MOE_GRAN, 1, bm // MOE_GRAN)
    kind = jnp.where(valid == 1, quarters, 0).astype(jnp.int32)
    return e, j, b, first, kind


def moe_ffn(h, layer, w_router, b_router, w_gu, b_gu, w_dn, b_dn):
    T, D = h.shape
    de = w_dn.shape[2]
    bm = MOE_BM
    idx128, gate128 = router(h, w_router[layer], b_router[layer], tm=T // 12)
    top_idx, gates = idx128[:, :TOP_K], gate128[:, :TOP_K]
    A = T * TOP_K
    flat_e = top_idx.reshape(A)
    ch = LANES
    assert A % ch == 0
    onehot = (flat_e[:, None] == jnp.arange(N_EXPERTS, dtype=jnp.int32)[None, :]).astype(f32).reshape(A // ch, ch, N_EXPERTS)
    within = jnp.einsum("ij,cjk->cik", jnp.tril(jnp.ones((ch, ch), f32)), onehot)
    chunk_tot = within[:, -1, :]
    before = jnp.cumsum(chunk_tot, axis=0) - chunk_tot
    counts = jnp.sum(chunk_tot, axis=0).astype(jnp.int32)
    rank = (jnp.sum((within + before[:, None, :]) * onehot, axis=2) - 1.0).astype(jnp.int32).reshape(A)
    nb_e = (counts + bm - 1) // bm
    blk_start = jnp.cumsum(nb_e) - nb_e
    dest = (blk_start[flat_e] * bm + rank).astype(jnp.int32)
    n_blocks_max = (A + N_EXPERTS * (bm - 1)) // bm
    P = (n_blocks_max + 1) * bm
    slot_tok = (jnp.arange(P, dtype=jnp.int32) % T).at[dest].set(jnp.arange(A, dtype=jnp.int32) // TOP_K)
    inv = dest.reshape(T, TOP_K)
    xb = h[slot_tok]

    ns = MOE_W_SPLIT

    def w_part(kdim, tn, q, j0):
        return pl.BlockSpec((None, None, kdim // ns, tn), lambda i, e, j, b, f, v: (layer, e[i], q, j[i] + j0))

    tn1 = 512
    nj1 = de // tn1
    items =_moe_items(nb_e, blk_start, counts, bm, n_blocks_max, nj1)
    b_gu4 = b_gu.reshape(b_gu.shape[0], N_EXPERTS, 1, 2 * de)
    act = pl.pallas_call(
        _moe_up_body,
        out_shape=jax.ShapeDtypeStruct((P, de), bf16),
        grid_spec=pltpu.PrefetchScalarGridSpec(
            num_scalar_prefetch=5,
            grid=(n_blocks_max * nj1,),
            in_specs=[pl.BlockSpec((bm, D), lambda i, e, j, b, f, v: (b[i], 0))]
                     + [w_part(D, tn1, q, 0) for q in range(ns)] + [w_part(D, tn1, q, nj1) for q in range(ns)]
                     + [pl.BlockSpec((None, None, 1, tn1), lambda i, e, j, b, f, v: (layer, e[i], 0, j[i])),
                        pl.BlockSpec((None, None, 1, tn1), lambda i, e, j, b, f, v: (layer, e[i], 0, j[i] + nj1))],
            out_specs=pl.BlockSpec((bm, tn1), lambda i, e, j, b, f, v: (b[i], j[i])),
            scratch_shapes=[pltpu.VMEM((D, tn1), bf16), pltpu.VMEM((D, tn1), bf16)]),
        compiler_params=_cparams("arbitrary"),
        name="moe_up",
    )(*items, xb, *([w_gu] * (2 * ns)), b_gu4, b_gu4)

    tn2 = 1024
    nj2 = D // tn2
    items2 = _moe_items(nb_e, blk_start, counts, bm, n_blocks_max, nj2)
    b_dn4 = b_dn.reshape(b_dn.shape[0], N_EXPERTS, 1, D)
    yb = pl.pallas_call(
        _moe_down_body,
        out_shape=jax.ShapeDtypeStruct((P, D), f32),
        grid_spec=pltpu.PrefetchScalarGridSpec(
            num_scalar_prefetch=5,
            grid=(n_blocks_max * nj2,),
            in_specs=[pl.BlockSpec((bm, de), lambda i, e, j, b, f, v: (b[i], 0))]
                     + [w_part(de, tn2, q, 0) for q in range(ns)]
                     + [pl.BlockSpec((None, None, 1, tn2), lambda i, e, j, b, f, v: (layer, e[i], 0, j[i]))],
            out_specs=pl.BlockSpec((bm, tn2), lambda i, e, j, b, f, v: (b[i], j[i])),
            scratch_shapes=[pltpu.VMEM((de, tn2), bf16)]),
        compiler_params=_cparams("arbitrary"),
        name="moe_down",
    )(*items2, act, *([w_dn] * ns), b_dn4)
    return sum(gates[:, k:k + 1] * yb[inv[:, k]] for k in range(TOP_K))


def _mixer(l, x, mod, past, W):
    B, L, D = x.shape
    M = B * L
    prompt = past is None
    sh1, sc1, g1 = mod[0], mod[1], mod[2]
    h = rms_norm_mod(x, W["norm_mix"][l], sc1, sh1).reshape(M, D)
    tm = 1024 if M % 1024 == 0 else M
    z = matmul_nt(h, W["w_in_main"], layer=l, tm=tm, tn=512 if prompt else 1536, name="in_proj")
    zs = matmul_nt(h, W["w_in_small"][l], tm=tm, tn=2 * LANES, name="in_proj_small")
    nz = z.shape[1]
    z3 = z.reshape(B, L, nz)

    if prompt:
        zqkv, zs3, tile = z3, zs.reshape(B, L, 2 * LANES), GDN_TILE
        buf8 = jnp.zeros((B, 8, GDN_CONV_W), f32)
        state0 = jnp.zeros((B, GDN_HEADS, HEAD_DIM, HEAD_DIM), f32)
        gate_arr, gate_col0 = z3, Z_GATE_A
    else:
        tile = lp = GDN_CHUNK
        assert L <= lp
        rpad = ((0, 0), (0, lp - L), (0, 0))
        zqkv = jnp.pad(z3[:, :, :GDN_CONV_W], rpad)
        zs3 = jnp.pad(zs.reshape(B, L, 2 * LANES), rpad)
        buf8 = jnp.pad(past["state_conv"][l], ((0, 0), (8 - (GDN_CONV - 1), 0), (0, 0)))
        state0 = past["state_gdn"][l]
        gate_arr, gate_col0 = jnp.pad(z3[:, :, Z_GATE_A:Z_GATE_A + GDN_W], rpad), 0
    qa, ka, va, gcum, beta = gdn_prep(zqkv, zs3, buf8, W["conv_w"][l], W["a_log"][l], W["dt_bias"][l], L, tile)
    w1, w2, qk, qg, kd = gdn_intra(qa, ka, va, gcum, beta, tile, min(GDN_CHUNK, L))
    o_a, gdn_new = gdn_seq(w1, w2, qg, kd, qk, gcum, gate_arr, gate_col0, state0, W["gdn_norm_w"][l], tile)
    o_a = o_a[:, :L].reshape(M, GDN_W)
    conv_new = (jnp.concatenate([buf8[:, 8 - (GDN_CONV - 1):], z3[:, :, :GDN_CONV_W]], axis=1))[:, -(GDN_CONV - 1):]

    lam_init = 0.8 - 0.6 * math.exp(-0.3 * l)
    if prompt:
        o_b, kb, vb = moba_prompt(z3, B, L)
        o_c, kc, vc = diff_prompt(z3, B, L, W["diff_lambda"][l], W["diff_norm_w"][l], lam_init)
        kb, vb, kc, vc = (t.transpose(0, 2, 1, 3) for t in (kb, vb, kc, vc))
    else:
        kb = z3[:, :, Z_KB:Z_KB + MOBA_W].reshape(B, L, MOBA_HEADS, HEAD_DIM)
        vb = z3[:, :, Z_VB:Z_VB + MOBA_W].reshape(B, L, MOBA_HEADS, HEAD_DIM)
        kc = z3[:, :, Z_KC:Z_KC + DIFF_W].reshape(B, L, 2 * DIFF_HEADS, HEAD_DIM)
        vc = z3[:, :, Z_VC:Z_VC + DIFF_W].reshape(B, L, DIFF_HEADS, 2 * HEAD_DIM)
        qb = z3[:, :, Z_QB:Z_QB + MOBA_W].reshape(B, L, MOBA_HEADS, HEAD_DIM)
        qc = z3[:, :, Z_QC:Z_QC + DIFF_W].reshape(B, L, 2 * DIFF_HEADS, HEAD_DIM)
        pt = past["page_table"]
        o_b = decode_attention("moba", l, pt, qb, kb, vb, past["cache_moba_k"], past["cache_moba_v"],
                               W["diff_lambda"][l], jnp.ones((HEAD_DIM,), f32), lam_init)
        o_c = decode_attention("diff", l, pt, qc, kc, vc, past["cache_diff_k"], past["cache_diff_v"],
                               W["diff_lambda"][l], W["diff_norm_w"][l], lam_init)
    o_b = o_b.reshape(M, MOBA_W).astype(bf16)
    o_c = o_c.reshape(M, DIFF_W).astype(bf16)

    wa, wb, wc = W["w_branch_split"][l]
    merged = branch_merge(o_a, o_b, o_c, wa, wb, wc, z, tm)
    x_new = proj_residual(merged, W["w_out_bf"][l], x.reshape(M, D), g1, L, tm).reshape(B, L, D)
    return x_new, (gdn_new, conv_new, kb, vb, kc, vc)


def kernel(x_prompt, x_sample, cache_moba_k, cache_moba_v, cache_diff_k, cache_diff_v, state_gdn, state_conv,
           page_table, c_prompt, c_sample, w_mod, b_mod, norm_mix, norm_ffn, w_in, conv_w, a_log, dt_bias,
           gdn_norm_w, diff_lambda, diff_norm_w, w_branch, w_out, w_router, b_router, w_gu, b_gu, w_dn, b_dn,
           norm_out):
    depth = w_in.shape[0]
    D = x_prompt.shape[-1]
    Bp, Lp, _ = x_prompt.shape
    Bs, Ls, _ = x_sample.shape
    n_dec = 2 * GDN_HEADS
    small0 = GDN_CONV_W

    w_in_main = repack_rows_bf16(jnp.swapaxes(w_in, 1, 2), small0, n_dec, 512)
    w_in_small = [jnp.zeros((2 * LANES, D), f32)
                  .at[:GDN_HEADS].set(w_in[l, :, small0:small0 + GDN_HEADS].T)
                  .at[LANES:LANES + GDN_HEADS].set(w_in[l, :, small0 + GDN_HEADS:small0 + n_dec].T)
                  for l in range(depth)]
    w_branch_split = [(w_branch[l][:GDN_W].astype(bf16), w_branch[l][GDN_W:GDN_W + MOBA_W].astype(bf16),
                       w_branch[l][GDN_W + MOBA_W:].astype(bf16)) for l in range(depth)]
    W = dict(norm_mix=norm_mix, norm_ffn=norm_ffn, w_in_main=w_in_main, w_in_small=w_in_small, conv_w=conv_w,
             a_log=a_log, dt_bias=dt_bias, gdn_norm_w=gdn_norm_w, diff_lambda=diff_lambda, diff_norm_w=diff_norm_w,
             w_branch_split=w_branch_split, w_out_bf=[w_out[l].astype(bf16) for l in range(depth)])
    head_major = lambda c: c.transpose(0, 1, 3, 2, 4)
    past = dict(page_table=page_table, state_gdn=state_gdn, state_conv=state_conv,
                cache_moba_k=head_major(cache_moba_k), cache_moba_v=head_major(cache_moba_v),
                cache_diff_k=head_major(cache_diff_k), cache_diff_v=head_major(cache_diff_v))

    c_all = jnp.concatenate([c_prompt, c_sample], axis=0)
    n_c = c_all.shape[0]
    c_act = jnp.pad(c_all * jax.nn.sigmoid(c_all), ((0, -n_c % 16), (0, 0)))
    xp, xs = x_prompt, x_sample
    rows_p, rows_s = [], []
    Tp, Ts = Bp * Lp, Bs * Ls
    for l in range(depth):
        mod = matmul_bias(c_act, w_mod, l, b_mod[l], tn=1024, k_split=4, name="adaln_mod")
        mod = jnp.split(mod[:n_c], N_MOD, axis=-1)
        mod_p = [m[:Bp] for m in mod]
        mod_s = [m[Bp:] for m in mod]
        xp, rp = _mixer(l, xp, mod_p, None, W)
        xs, rs = _mixer(l, xs, mod_s, past, W)
        rows_p.append(rp)
        rows_s.append(rs)
        hp = rms_norm_mod(xp, norm_ffn[l], mod_p[4], mod_p[3]).reshape(Tp, D)
        hs = rms_norm_mod(xs, norm_ffn[l], mod_s[4], mod_s[3]).reshape(Ts, D)
        y = moe_ffn(jnp.concatenate([hp, hs], axis=0), l, w_router, b_router, w_gu, b_gu, w_dn, b_dn)
        xp = xp + mod_p[5][:, None, :] * y[:Tp].reshape(Bp, Lp, D)
        xs = xs + mod_s[5][:, None, :] * y[Tp:].reshape(Bs, Ls, D)
    yp = rms_norm_mod(xp, norm_out, out_dtype=f32)
    ys = rms_norm_mod(xs, norm_out, out_dtype=f32)
    stack = lambda rows, i: jnp.stack([r[i] for r in rows])
    return ((yp, ys) + tuple(stack(rows_p, i) for i in range(6)) + tuple(stack(rows_s, i) for i in range(6)))
```

```python
import functools
import math

import jax
import jax.numpy as jnp
from jax import lax
from jax.experimental import pallas as pl
from jax.experimental.pallas import tpu as pltpu

f32 = jnp.float32
bf16 = jnp.bfloat16

HEAD_DIM = 128
GDN_HEADS = 12
GDN_CONV = 4
GDN_CHUNK = 64
GDN_W = GDN_HEADS * HEAD_DIM
GDN_CONV_W = 3 * GDN_W
MOBA_HEADS = 10
MOBA_BLOCK = 256
MOBA_TOPK = 3
MOBA_W = MOBA_HEADS * HEAD_DIM
DIFF_HEADS = 5
DIFF_W = 2 * DIFF_HEADS * HEAD_DIM
N_EXPERTS = 32
TOP_K = 4
SWIGLU_LIMIT = 7.0
SWIGLU_ALPHA = 1.702
N_MOD = 6
RMS_EPS = 1e-6
L2_EPS = 1e-6
PAGE_SIZE = 128

LANES = 128
GDN_TILE = 256
GDN_HEAD_GROUP = 4
GDN_INTRA_HEADS = 2
ATT_QTILE = 256
DEC_PAGES_PER_STEP = 8
MOE_BM = 512
MOE_GRAN = 128
MOE_W_SPLIT = 2
VMEM_LIMIT_BYTES =56 * 1024 * 1024

Z_GATE_A = GDN_CONV_W
Z_QB = Z_GATE_A + GDN_W
Z_KB = Z_QB + MOBA_W
Z_VB = Z_KB + MOBA_W
Z_QC = Z_VB + MOBA_W
Z_KC = Z_QC + DIFF_W
Z_VC = Z_KC + DIFF_W
Z_BR = Z_VC + DIFF_W


def _cparams(*sem):
    return pltpu.CompilerParams(dimension_semantics=sem, vmem_limit_bytes=VMEM_LIMIT_BYTES)


def _dot(a, b):
    return jnp.dot(a, b, preferred_element_type=f32)


def _dot_nt(a, b, precision=None):
    return lax.dot_general(a, b, (((1,), (1,)), ((), ())), preferred_element_type=f32, precision=precision)


def _dot_tn(a, b):
    return lax.dot_general(a, b, (((0,), (0,)), ((), ())), preferred_element_type=f32)


def _sigmoid(x):
    return 1.0 / (1.0 + jnp.exp(-x))


def _norm_body(x_ref, w_ref, *rest, modulate):
    o_ref = rest[-1]
    x = x_ref[0]
    y = x * lax.rsqrt(jnp.mean(x * x, axis=-1, keepdims=True) + RMS_EPS) * w_ref[...]
    if modulate:
        sc_ref, sh_ref = rest[0], rest[1]
        y = y * (1.0 + sc_ref[0]) + sh_ref[0]
    o_ref[0] = y.astype(o_ref.dtype)


def rms_norm_mod(x, w, sc=None, sh=None, out_dtype=bf16):
    B, L, D = x.shape
    tl = min(L, 512)
    modulate = sc is not None
    in_specs = [pl.BlockSpec((1, tl, D), lambda b, i: (b, i, 0)),
                pl.BlockSpec((1, D), lambda b, i: (0, 0))]
    args = [x, w.reshape(1, D)]
    if modulate:
        in_specs += [pl.BlockSpec((1, 1, D), lambda b, i: (b, 0, 0))] * 2
        args += [sc.reshape(B, 1, D), sh.reshape(B, 1, D)]
    return pl.pallas_call(
        functools.partial(_norm_body, modulate=modulate),
        out_shape=jax.ShapeDtypeStruct((B, L, D), out_dtype),
        grid=(B, L // tl),
        in_specs=in_specs,
        out_specs=pl.BlockSpec((1, tl, D), lambda b, i: (b, i, 0)),
        compiler_params=_cparams("parallel", "parallel"),
        name="rms_norm_mod",
    )(*args)


def _mm_bias_body(a_ref, *rest, k_split):
    w_refs, bias_ref, o_ref = rest[:k_split], rest[k_split], rest[k_split + 1]
    kq = w_refs[0].shape[0]
    acc = bias_ref[...]
    for q, w_ref in enumerate(w_refs):
        acc = acc + _dot(a_ref[:, q * kq:(q + 1) * kq].astype(bf16), w_ref[...].astype(bf16))
    o_ref[...] = acc


def matmul_bias(a, w, layer, bias, *, tn, k_split, name):
    M, K = a.shape
    N = w.shape[-1]
    assert N % tn == 0 and K % k_split == 0
    w_specs = [pl.BlockSpec((None, K // k_split, tn), functools.partial(lambda q, j: (layer, q, j), q))
               for q in range(k_split)]
    return pl.pallas_call(
        functools.partial(_mm_bias_body, k_split=k_split),
        out_shape=jax.ShapeDtypeStruct((M, N), f32),
        grid=(N // tn,),
        in_specs=[pl.BlockSpec((M, K), lambda j: (0, 0))] + w_specs + [pl.BlockSpec((1, tn), lambda j: (0, j))],
        out_specs=pl.BlockSpec((M, tn), lambda j: (0, j)),
        compiler_params=_cparams("parallel"),
        name=name,
    )(a, *([w] * k_split), bias.reshape(1, N))


def _mm_nt_body(a_ref, w_ref, o_ref):
    w = w_ref[0] if len(w_ref.shape) == 3 else w_ref[...]
    o_ref[...] = _dot_nt(a_ref[...].astype(bf16), w.astype(bf16)).astype(o_ref.dtype)


def matmul_nt(a, wt, *, tm, tn, layer=None, out_dtype=f32, name="matmul_nt"):
    M, K = a.shape
    N = wt.shape[-2]
    assert M % tm == 0 and N % tn == 0
    if layer is None:
        w_spec = pl.BlockSpec((tn, K), lambda i, j: (j, 0))
    else:
        w_spec = pl.BlockSpec((1, tn, K), lambda i, j: (layer, j, 0))
    return pl.pallas_call(
        _mm_nt_body,
        out_shape=jax.ShapeDtypeStruct((M, N), out_dtype),
        grid=(M // tm, N // tn),
        in_specs=[pl.BlockSpec((tm, K), lambda i, j: (i, 0)), w_spec],
        out_specs=pl.BlockSpec((tm, tn), lambda i, j: (i, j)),
        compiler_params=_cparams("parallel", "parallel"),
        name=name,
    )(a, wt)


def _repack_body(lo_ref, hi_ref, o_ref):
    half = lo_ref.shape[1]
    o_ref[0, :half, :] = lo_ref[0].astype(o_ref.dtype)
    o_ref[0, half:, :] = hi_ref[0].astype(o_ref.dtype)


def repack_rows_bf16(wt, start, width, tn):
    depth, n_all, K = wt.shape
    N = n_all - width
    sub = 8
    assert N % tn == 0 and start % tn == 0 and width % sub == 0 and tn % (2 * sub) == 0
    first_after = start // tn

    def rows(j, half):
        return (j * (tn // sub) + half * (tn // (2 * sub)) + jnp.where(j >= first_after, width // sub, 0)) * sub

    part = lambda half: pl.BlockSpec((pl.Element(1), pl.Element(tn // 2), pl.Element(K)),
                                     lambda l, j: (l, rows(j, half), 0))
    return pl.pallas_call(
        _repack_body,
        out_shape=jax.ShapeDtypeStruct((depth, N, K), bf16),
        grid=(depth, N // tn),
        in_specs=[part(0), part(1)],
        out_specs=pl.BlockSpec((1, tn, K), lambda l, j: (l, j, 0)),
        compiler_params=_cparams("parallel", "parallel"),
        name="repack_rows_bf16",
    )(wt, wt)


def _softplus(x):
    return jnp.maximum(x, 0.0) + jnp.log1p(jnp.exp(-jnp.abs(x)))


def _gdn_prep_body(x_ref, prev_ref, buf_ref, cw_ref, zs_ref, al_ref, dtb_ref,
                   q_ref, k_ref, v_ref, g_ref, beta_ref, *, tl, valid_len):
    i = pl.program_id(1)
    row = lax.broadcasted_iota(jnp.int32, (tl, 1), 0)
    valid = (i * tl + row) < valid_len
    row8 = lax.broadcasted_iota(jnp.int32, (8, LANES), 0)
    first = i == 0
    outs = (q_ref, k_ref, v_ref)
    for part in range(3):
        for h in range(GDN_HEADS):
            c0 = part * GDN_W + h * HEAD_DIM
            cols = slice(c0, c0 + HEAD_DIM)
            x = x_ref[0, :, cols]
            prev = jnp.where(first, buf_ref[0, :, cols], prev_ref[0, :, cols])
            y = x * cw_ref[GDN_CONV - 1:GDN_CONV, cols]
            for s in range(1, GDN_CONV):
                xs = pltpu.roll(x, s, axis=0)
                top = jnp.where(row8 < s, pltpu.roll(prev, s, axis=0), xs[:8])
                xs = jnp.concatenate([top, xs[8:]], axis=0)
                y = y + xs * cw_ref[GDN_CONV - 1 - s:GDN_CONV - s, cols]
            y = y * _sigmoid(y)
            if part < 2:
                y = y * lax.rsqrt(jnp.sum(y * y, axis=-1, keepdims=True) + L2_EPS)
            if part == 0:
                y = y * (HEAD_DIM ** -0.5)
            outs[part][0, :, h * HEAD_DIM:(h + 1) * HEAD_DIM] = jnp.where(valid, y, 0.0)
    zs = zs_ref[0]
    g = -jnp.exp(al_ref[...]) * _softplus(zs[:, :LANES] + dtb_ref[...])
    g = jnp.where(valid, g, 0.0)
    beta_ref[0] = jnp.where(valid, _sigmoid(zs[:, LANES:]), 0.0)
    in_chunk = row % GDN_CHUNK
    s = 1
    while s < GDN_CHUNK:
        g = g + jnp.where(in_chunk >= s, pltpu.roll(g, s, axis=0), 0.0)
        s *= 2
    g_ref[0] = g


def gdn_prep(zqkv, zs, buf8, conv_w, a_log, dt_bias, valid_len, tl):
    B, L, _ = zqkv.shape
    assert L % tl == 0 and tl % GDN_CHUNK == 0
    nprev = tl // 8
    pad = lambda t: jnp.pad(t.astype(f32), (0, LANES - GDN_HEADS)).reshape(1, LANES)
    row_spec = lambda w: pl.BlockSpec((1, tl, w), lambda b, i: (b, i, 0))
    outs = pl.pallas_call(
        functools.partial(_gdn_prep_body, tl=tl, valid_len=valid_len),
        out_shape=[jax.ShapeDtypeStruct((B, L, GDN_W), f32)] * 3 + [jax.ShapeDtypeStruct((B, L, LANES), f32)] * 2,
        grid=(B, L // tl),
        in_specs=[row_spec(GDN_CONV_W),
                  pl.BlockSpec((1, 8, GDN_CONV_W), lambda b, i: (b, jnp.maximum(i * nprev - 1, 0), 0)),
                  pl.BlockSpec((1, 8, GDN_CONV_W), lambda b, i: (b, 0, 0)),
                  pl.BlockSpec((GDN_CONV, GDN_CONV_W), lambda b, i: (0, 0)),
                  row_spec(2 * LANES),
                  pl.BlockSpec((1, LANES), lambda b, i: (0, 0)),
                  pl.BlockSpec((1, LANES), lambda b, i: (0, 0))],
        out_specs=[row_spec(GDN_W)] * 3 + [row_spec(LANES)] * 2,
        compiler_params=_cparams("parallel", "parallel"),
        name="gdn_prep",
    )(zqkv, zqkv, buf8, conv_w, zs, pad(a_log), pad(dt_bias))
    return outs


def _split_bf16(a):
    hi = a.astype(bf16)
    return hi, (a - hi.astype(f32)).astype(bf16)


def _dot_split(a, b):
    return _dot(a[0], b[0]) + _dot(a[0], b[1]) + _dot(a[1], b[0])


def _gdn_intra_body(q_ref, k_ref, v_ref, g_ref, beta_ref,
                    w1_ref, w2_ref, qk_ref, qg_ref, kd_ref, *, tl, nil):
    nh = GDN_INTRA_HEADS
    hg = pl.program_id(1)
    lane = lax.broadcasted_iota(jnp.int32, (tl, LANES), 1)
    r = lax.broadcasted_iota(jnp.int32, (tl, tl), 0)
    c = lax.broadcasted_iota(jnp.int32, (tl, tl), 1)
    same = (r // GDN_CHUNK) == (c // GDN_CHUNK)
    eye = r == c
    is_last = c == (r // GDN_CHUNK) * GDN_CHUNK + (GDN_CHUNK - 1)
    heads = []
    for hh in range(nh):
        cols = slice(hh * HEAD_DIM, (hh + 1) * HEAD_DIM)
        h = hg * nh + hh
        gcol = jnp.sum(jnp.where(lane == h, g_ref[0], 0.0), axis=-1, keepdims=True)
        bcol = jnp.sum(jnp.where(lane == h, beta_ref[0], 0.0), axis=-1, keepdims=True)
        grow = jnp.sum(jnp.where(eye, gcol, 0.0), axis=0, keepdims=True)
        decay = jnp.exp(jnp.where(same & (r >= c), gcol - grow, -jnp.inf))
        k = k_ref[0, :, cols]
        q = q_ref[0, :, cols]
        kb = k.astype(bf16)
        n = jnp.where(same & (r > c), -(bcol * _dot_nt(kb, kb) * decay), 0.0)
        gam = jnp.exp(gcol)
        qk_ref[0, hh] = _dot_nt(q.astype(bf16), kb) * decay
        qg_ref[0, :, cols] = q * gam
        glast = jnp.sum(jnp.where(is_last, grow, 0.0), axis=-1, keepdims=True)
        kd_ref[0, :, cols] = k * jnp.exp(glast - gcol)
        rhs = jnp.concatenate([bcol * v_ref[0, :, cols], (bcol * gam) * k], axis=1)
        heads.append(dict(t=jnp.where(eye, 1.0, 0.0) + n, p=_split_bf16(n), rhs=_split_bf16(rhs)))
    span = 2
    while span < nil:
        for st in heads:
            st["p"] = _split_bf16(_dot_split(st["p"], st["p"]))
        for st in heads:
            st["t"] = st["t"] + _dot_split(_split_bf16(st["t"]), st["p"])
        span *= 2
    ws = [_dot_split(_split_bf16(st["t"]), st["rhs"]) for st in heads]
    w1_ref[0] = jnp.concatenate([w[:, :HEAD_DIM] for w in ws], axis=1)
    w2_ref[0] = jnp.concatenate([w[:, HEAD_DIM:] for w in ws], axis=1)


def gdn_intra(q, k, v, g, beta, tl, nil):
    B, L, _ = q.shape
    nh = GDN_INTRA_HEADS
    head_spec = pl.BlockSpec((1, tl, nh * HEAD_DIM), lambda b, h, i: (b, i, h))
    all_spec = pl.BlockSpec((1, tl, LANES), lambda b, h, i: (b, i, 0))
    wide = jax.ShapeDtypeStruct((B, L, GDN_W), f32)
    return pl.pallas_call(
        functools.partial(_gdn_intra_body, tl=tl, nil=nil),
        out_shape=[wide, wide, jax.ShapeDtypeStruct((B, GDN_HEADS, L, tl), f32), wide, wide],
        grid=(B, GDN_HEADS // nh, L // tl),
        in_specs=[head_spec, head_spec, head_spec, all_spec, all_spec],
        out_specs=[head_spec, head_spec,
                   pl.BlockSpec((1, nh, tl, tl), lambda b, h, i: (b, h, i, 0)),
                   head_spec, head_spec],
        compiler_params=_cparams("parallel", "parallel", "parallel"),
        name="gdn_intra",
    )(q, k, v, g, beta)


def _gdn_seq_body(w1_ref, w2_ref, qg_ref, kd_ref, qk_ref, g_ref, gate_ref, s0_ref, nw_ref,
                  o_ref, so_ref, s_scr, u_scr, *, tl):
    hg = pl.program_id(1)
    ti = pl.program_id(2)

    @pl.when(ti == 0)
    def _():
        s_scr[...] = s0_ref[0]

    u_scr[...] = jnp.zeros_like(u_scr)
    lane = lax.broadcasted_iota(jnp.int32, (1, LANES), 1)
    for ci in range(tl // GDN_CHUNK):
        rows = slice(ci * GDN_CHUNK, (ci + 1) * GDN_CHUNK)
        last = ci * GDN_CHUNK + GDN_CHUNK - 1
        g_last_row = g_ref[0, last:last + 1, :]
        for hh in range(GDN_HEAD_GROUP):
            cols = slice(hh * HEAD_DIM, (hh + 1) * HEAD_DIM)
            s = s_scr[hh]
            sb = s.astype(bf16)
            u = w1_ref[0, rows, cols] - _dot(w2_ref[0, rows, cols].astype(bf16), sb)
            u_scr[hh, rows, :] = u
            o = _dot(qg_ref[0, rows, cols].astype(bf16), sb) + _dot(qk_ref[0, hh, rows, :].astype(bf16),
                                                                  u_scr[hh].astype(bf16))
            g_last = jnp.sum(jnp.where(lane == hg * GDN_HEAD_GROUP + hh, g_last_row, 0.0), axis=-1, keepdims=True)
            s_scr[hh] = jnp.exp(g_last) * s + _dot_tn(kd_ref[0, rows, cols].astype(bf16), u.astype(bf16))
            y = o * lax.rsqrt(jnp.mean(o * o, axis=-1, keepdims=True) + RMS_EPS) * nw_ref[...]
            gate = gate_ref[0, rows, cols]
            o_ref[0, rows, cols] = (y * (gate * _sigmoid(gate))).astype(o_ref.dtype)

    @pl.when(ti == pl.num_programs(2) - 1)
    def _():
        so_ref[0] = s_scr[...]


def gdn_seq(w1, w2, qg, kd, qk, g, gate_arr, gate_col0, state0, norm_w, tl):
    B, L, _ = w1.shape
    gw = GDN_HEAD_GROUP * HEAD_DIM
    assert gate_col0 % gw == 0
    gcb = gate_col0 // gw
    grp = pl.BlockSpec((1, tl, gw), lambda b, hg, i: (b, i, hg))
    return pl.pallas_call(
        functools.partial(_gdn_seq_body, tl=tl),
        out_shape=[jax.ShapeDtypeStruct((B, L, GDN_W), bf16),
                   jax.ShapeDtypeStruct((B, GDN_HEADS, HEAD_DIM, HEAD_DIM), f32)],
        grid=(B, GDN_HEADS // GDN_HEAD_GROUP, L // tl),
        in_specs=[grp, grp, grp, grp,
                  pl.BlockSpec((1, GDN_HEAD_GROUP, tl, tl), lambda b, hg, i: (b, hg, i, 0)),
                  pl.BlockSpec((1, tl, LANES), lambda b, hg, i: (b, i, 0)),
                  pl.BlockSpec((1, tl, gw), lambda b, hg, i: (b, i, gcb + hg)),
                  pl.BlockSpec((1, GDN_HEAD_GROUP, HEAD_DIM, HEAD_DIM), lambda b, hg, i: (b, hg, 0, 0)),
                  pl.BlockSpec((1, HEAD_DIM), lambda b, hg, i: (0, 0))],
        out_specs=[grp,
                   pl.BlockSpec((1, GDN_HEAD_GROUP, HEAD_DIM, HEAD_DIM), lambda b, hg, i: (b, hg, 0, 0))],
        scratch_shapes=[pltpu.VMEM((GDN_HEAD_GROUP, HEAD_DIM, HEAD_DIM), f32),
                        pltpu.VMEM((GDN_HEAD_GROUP, tl, HEAD_DIM), f32)],
        compiler_params=_cparams("parallel", "parallel", "arbitrary"),
        name="gdn_seq",
    )(w1, w2, qg, kd, qk, g, gate_arr, state0, norm_w.reshape(1, HEAD_DIM))


def _topk_mask(score, lane_or_row, n_valid, axis, size):
    sel = jnp.zeros(score.shape, jnp.bool_)
    for r in range(MOBA_TOPK):
        m = jnp.max(score, axis=axis, keepdims=True)
        idx = jnp.min(jnp.where(score == m, lane_or_row, size), axis=axis, keepdims=True)
        hit = lane_or_row == idx
        sel = sel | (hit & (r < n_valid))
        score = jnp.where(hit, -jnp.inf, score)
    return sel


def _moba_prompt_body(q_ref, k_ref, v_ref, o_ref, ko_ref, vo_ref, kmean_scr, *, seq):
    nblk = seq // MOBA_BLOCK

    @pl.when(pl.program_id(2) == 0)
    def _():
        k = k_ref[0]
        ko_ref[0, 0] = k
        vo_ref[0, 0] = v_ref[0]
        kmean = jnp.mean(k.reshape(nblk, MOBA_BLOCK, HEAD_DIM), axis=1)
        kmean_scr[...] = jnp.concatenate([kmean, jnp.zeros((LANES - nblk, HEAD_DIM), f32)], axis=0)

    for n in range(nblk):
        pl.when(pl.program_id(2) == n)(
            functools.partial(_moba_prompt_tile, q_ref, k_ref, v_ref, o_ref, kmean_scr, n))


def _moba_prompt_tile(q_ref, k_ref, v_ref, o_ref, kmean_scr, i):
    tq = ATT_QTILE
    seq = (i + 1) * MOBA_BLOCK
    q = q_ref[0]
    k = k_ref[0, :seq, :]
    lane = lax.broadcasted_iota(jnp.int32, (tq, LANES), 1)
    score = _dot_nt(q, kmean_scr[...], precision=lax.Precision.HIGHEST)
    score = jnp.where(lane < i, score, -jnp.inf)
    sel = _topk_mask(score, lane, i, 1, LANES)
    key = lax.broadcasted_iota(jnp.int32, (LANES, seq), 1)
    blk_row = lax.broadcasted_iota(jnp.int32, (LANES, seq), 0)
    member = jnp.where(key // MOBA_BLOCK == blk_row, 1.0, 0.0).astype(bf16)
    sel_keys = _dot(jnp.where(sel, 1.0, 0.0).astype(bf16), member) > 0.5
    kpos = lax.broadcasted_iota(jnp.int32, (tq, seq), 1)
    qpos = i * tq + lax.broadcasted_iota(jnp.int32, (tq, seq), 0)
    own = (kpos // MOBA_BLOCK == i) & (kpos <= qpos)
    s = _dot_nt(q.astype(bf16), k.astype(bf16)) * (HEAD_DIM ** -0.5)
    s = jnp.where(sel_keys | own, s, -jnp.inf)
    p = jnp.exp(s - jnp.max(s, axis=-1, keepdims=True))
    p = p / jnp.sum(p, axis=-1, keepdims=True)
    o_ref[0] = _dot(p.astype(bf16), v_ref[0, :seq, :].astype(bf16)).astype(o_ref.dtype)


def moba_prompt(z, B, L):
    tq = ATT_QTILE
    assert L % tq == 0 and tq == MOBA_BLOCK
    qb, kb, vb = Z_QB // HEAD_DIM, Z_KB // HEAD_DIM, Z_VB // HEAD_DIM
    rows = jax.ShapeDtypeStruct((B, MOBA_HEADS, L, HEAD_DIM), f32)
    rows_spec = pl.BlockSpec((1, 1, L, HEAD_DIM), lambda b, h, i: (b, h, 0, 0))
    return pl.pallas_call(
        functools.partial(_moba_prompt_body, seq=L),
        out_shape=[jax.ShapeDtypeStruct((B, L, MOBA_W), bf16), rows, rows],
        grid=(B, MOBA_HEADS, L // tq),
        in_specs=[pl.BlockSpec((1, tq, HEAD_DIM), lambda b, h, i: (b, i, qb + h)),
                  pl.BlockSpec((1, L, HEAD_DIM), lambda b, h, i: (b, 0, kb + h)),
                  pl.BlockSpec((1, L, HEAD_DIM), lambda b, h, i: (b, 0, vb + h))],
        out_specs=[pl.BlockSpec((1, tq, HEAD_DIM), lambda b, h, i: (b, i, h)), rows_spec, rows_spec],
        scratch_shapes=[pltpu.VMEM((LANES, HEAD_DIM), f32)],
        compiler_params=_cparams("parallel", "parallel", "arbitrary"),
        name="moba_prompt",
    )(z, z, z)


def _diff_lambda(dl_ref, lam_init):
    dl = dl_ref[...]
    a = jnp.sum(dl[0:1] * dl[1:2], axis=-1, keepdims=True)
    b = jnp.sum(dl[2:3] * dl[3:4], axis=-1, keepdims=True)
    return jnp.exp(a) - jnp.exp(b) + lam_init


def _diff_prompt_body(q_ref, k_ref, v_ref, dl_ref, nw_ref, o_ref, ko_ref, vo_ref, *, seq, lam_init):
    @pl.when(pl.program_id(2) == 0)
    def _():
        ko_ref[0, 0] = k_ref[0, :, :HEAD_DIM]
        ko_ref[0, 1] = k_ref[0, :, HEAD_DIM:]
        vo_ref[0, 0] = v_ref[0]

    for n in range(seq // ATT_QTILE):
        pl.when(pl.program_id(2) == n)(
            functools.partial(_diff_prompt_tile, q_ref, k_ref, v_ref, dl_ref, nw_ref, o_ref, n, lam_init))


def _diff_prompt_tile(q_ref, k_ref, v_ref, dl_ref, nw_ref, o_ref, i, lam_init):
    tq = ATT_QTILE
    seq = (i + 1) * tq
    lam = _diff_lambda(dl_ref, lam_init)
    kpos = lax.broadcasted_iota(jnp.int32, (tq, seq), 1)
    qpos = i * tq + lax.broadcasted_iota(jnp.int32, (tq, seq), 0)
    causal = kpos <= qpos
    ps = []
    for m in range(2):
        cols = slice(m * HEAD_DIM, (m + 1) * HEAD_DIM)
        s = _dot_nt(q_ref[0, :, cols].astype(bf16), k_ref[0, :seq, cols].astype(bf16)) * (HEAD_DIM ** -0.5)
        s = jnp.where(causal, s, -jnp.inf)
        p = jnp.exp(s - jnp.max(s, axis=-1, keepdims=True))
        ps.append(p / jnp.sum(p, axis=-1, keepdims=True))
    w = ps[0] - lam * ps[1]
    o = _dot(w.astype(bf16), v_ref[0, :seq, :].astype(bf16))
    y = o * lax.rsqrt(jnp.mean(o * o, axis=-1, keepdims=True) + RMS_EPS) * nw_ref[...]
    o_ref[0] = (y * (1.0 - lam_init)).astype(o_ref.dtype)


def diff_prompt(z, B, L, diff_lambda, norm_w, lam_init):
    tq = ATT_QTILE
    hw = 2 * HEAD_DIM
    qb, kb, vb = Z_QC // hw, Z_KC // hw, Z_VC // hw
    return pl.pallas_call(
        functools.partial(_diff_prompt_body, seq=L, lam_init=lam_init),
        out_shape=[jax.ShapeDtypeStruct((B, L, DIFF_W), bf16),
                   jax.ShapeDtypeStruct((B, 2 * DIFF_HEADS, L, HEAD_DIM), f32),
                   jax.ShapeDtypeStruct((B, DIFF_HEADS, L, hw), f32)],
        grid=(B, DIFF_HEADS, L // tq),
        in_specs=[pl.BlockSpec((1, tq, hw), lambda b, h, i: (b, i, qb + h)),
                  pl.BlockSpec((1, L, hw), lambda b, h, i: (b, 0, kb + h)),
                  pl.BlockSpec((1, L, hw), lambda b, h, i: (b, 0, vb + h)),
                  pl.BlockSpec((4, HEAD_DIM), lambda b, h, i: (0, 0)),
                  pl.BlockSpec((1, hw), lambda b, h, i: (0, 0))],
        out_specs=[pl.BlockSpec((1, tq, hw), lambda b, h, i: (b, i, h)),
                   pl.BlockSpec((1, 2, L, HEAD_DIM), lambda b, h, i: (b, h, 0, 0)),
                   pl.BlockSpec((1, 1, L, hw), lambda b, h, i: (b, h, 0, 0))],
        compiler_params=_cparams("parallel", "parallel", "arbitrary"),
        name="diff_prompt",
    )(z, z, z, diff_lambda, norm_w.reshape(1, hw))


def _decode_attn_body(pt_ref, qx_ref, knew_ref, vnew_ref, dl_ref, nw_ref, *rest,
                      mode, n_steps, n_new, lam_init):
    pp = DEC_PAGES_PER_STEP
    k_refs = rest[:pp]
    v_refs = rest[pp:2 * pp]
    o_ref = rest[2 * pp]
    s_scr, bsum_scr, bmax_scr, sel_scr, inv_scr, pnew_scr, acc_scr = rest[2 * pp + 1:]
    ph = pl.program_id(1)
    g = pl.program_id(2)
    rows_step = pp * PAGE_SIZE
    blk_step = rows_step // MOBA_BLOCK
    nblk = n_steps * blk_step
    n_kh = MOBA_HEADS
    hv = acc_scr.shape[0]
    kh_per_v = n_kh // hv
    scale = HEAD_DIM ** -0.5

    def scores(load_head):
        acc = None
        for hp in range(n_kh // 2):
            lhs = jnp.concatenate([load_head(2 * hp), load_head(2 * hp + 1)], axis=1).astype(bf16)
            d = _dot_nt(lhs, qx_ref[0, hp])
            acc = d if acc is None else acc + d
        return acc

    @pl.when(ph == 0)
    def _():
        s = scores(lambda h: jnp.concatenate([k_refs[j][h] for j in range(pp)], axis=0))
        s_scr[pl.ds(pl.multiple_of(g * rows_step, rows_step), rows_step), :] = s
        for j in range(blk_step):
            blk = s[j * MOBA_BLOCK:(j + 1) * MOBA_BLOCK]
            bsum_scr[pl.ds(g * blk_step + j, 1), :] = jnp.mean(blk, axis=0, keepdims=True)
            bmax_scr[pl.ds(g * blk_step + j, 1), :] = jnp.max(blk, axis=0, keepdims=True)

    @pl.when((ph == 1) & (g == 0))
    def _():
        col = lax.broadcasted_iota(jnp.int32, (LANES, LANES), 1)
        krow = lax.broadcasted_iota(jnp.int32, (LANES, LANES), 0)
        new_ok = (krow < n_new) & (krow <= col % n_new)
        s_new = jnp.where(new_ok, scores(lambda h: knew_ref[0, h]) * scale, -jnp.inf)
        bmax = bmax_scr[...]
        if mode == "moba":
            brow = lax.broadcasted_iota(jnp.int32, (nblk, LANES), 0)
            sel = _topk_mask(bsum_scr[...], brow, nblk, 0, nblk)
            sel_scr[...] = jnp.where(sel, 1.0, 0.0)
            bmax = jnp.where(sel, bmax, -jnp.inf)
        m = jnp.maximum(jnp.max(bmax, axis=0, keepdims=True) * scale, jnp.max(s_new, axis=0, keepdims=True))

        def exp_block(b, tot):
            r0 = pl.multiple_of(b * MOBA_BLOCK, MOBA_BLOCK)
            s = s_scr[pl.ds(r0, MOBA_BLOCK), :] * scale
            if mode == "moba":
                s = jnp.where(sel_scr[pl.ds(b, 1), :] > 0.5, s, -jnp.inf)
            p = jnp.exp(s - m)
            s_scr[pl.ds(r0, MOBA_BLOCK), :] = p
            return tot + jnp.sum(p, axis=0, keepdims=True)

        p_new = jnp.exp(s_new - m)
        tot = lax.fori_loop(0, nblk, exp_block, jnp.sum(p_new, axis=0, keepdims=True))
        inv = 1.0 / tot
        inv_scr[...] = jnp.broadcast_to(inv, inv_scr.shape)
        pnew_scr[...] = p_new * inv
        acc_scr[...] = jnp.zeros_like(acc_scr)

    def accumulate(p, load_vhead):
        pt = p.T.astype(bf16)
        for h in range(hv):
            acc_scr[h] += _dot(pt, load_vhead(h).astype(bf16))

    @pl.when(ph == 1)
    def _():
        p = s_scr[pl.ds(pl.multiple_of(g * rows_step, rows_step), rows_step), :] * inv_scr[0:1, :]
        accumulate(p, lambda h: jnp.concatenate([v_refs[j][h] for j in range(pp)], axis=0))

    @pl.when((ph == 1) & (g == n_steps - 1))
    def _():
        accumulate(pnew_scr[...], lambda h: vnew_ref[0, h])
        dv = acc_scr.shape[2]
        rows_h = kh_per_v * n_new
        if mode == "diff":
            lam = _diff_lambda(dl_ref, lam_init)
        for h in range(hv):
            a = acc_scr[h, h * rows_h:(h + 1) * rows_h, :]
            if mode == "diff":
                o = a[:n_new] - lam * a[n_new:]
                y = o * lax.rsqrt(jnp.mean(o * o, axis=-1, keepdims=True) + RMS_EPS) * nw_ref[...]
                a = y * (1.0 - lam_init)
            o_ref[0, :, h * dv:(h + 1) * dv] = a.astype(o_ref.dtype)


def decode_attention(mode, layer, page_table, q, k_new, v_new, cache_k, cache_v,
                     diff_lambda, norm_w, lam_init):
    B, n_new, n_kh, _ = q.shape
    hv, dv = v_new.shape[2], v_new.shape[3]
    n_pages = page_table.shape[1]
    pp = DEC_PAGES_PER_STEP
    assert n_pages % pp == 0 and n_new == 8 and n_kh == MOBA_HEADS
    n_steps = n_pages // pp
    n_keys = n_pages * PAGE_SIZE
    eye = jnp.eye(n_kh, dtype=f32)
    qx = jnp.einsum("bqhd,hg->bhgqd", q, eye).reshape(B, n_kh, n_kh * n_new, HEAD_DIM)
    qx = jnp.pad(qx, ((0, 0), (0, 0), (0, LANES - n_kh * n_new), (0, 0)))
    qx = qx.reshape(B, n_kh // 2, 2, LANES, HEAD_DIM).transpose(0, 1, 3, 2, 4).reshape(B, n_kh // 2, LANES, 2 * HEAD_DIM)
    qx = qx.astype(bf16)
    pad_rows = lambda t: jnp.pad(t.transpose(0, 2, 1, 3), ((0, 0), (0, 0), (0, PAGE_SIZE - n_new), (0, 0)))
    knew, vnew = pad_rows(k_new), pad_rows(v_new)

    def k_map(j):
        return lambda b, ph, g, pt: (layer, pt[b, (g * (1 - ph) + (n_steps - 1) * ph) * pp + j], 0, 0, 0)

    def v_map(j):
        return lambda b, ph, g, pt: (layer, pt[b, g * ph * pp + j], 0, 0, 0)

    per_b = lambda *shape: pl.BlockSpec((1,) + shape, lambda b, ph, g, pt: (b,) + (0,) * len(shape))
    grid_spec = pltpu.PrefetchScalarGridSpec(
        num_scalar_prefetch=1,
        grid=(B, 2, n_steps),
        in_specs=[per_b(n_kh // 2, LANES, 2 * HEAD_DIM),
                  per_b(n_kh, PAGE_SIZE, HEAD_DIM),
                  per_b(hv, PAGE_SIZE, dv),
                  pl.BlockSpec((4, HEAD_DIM), lambda b, ph, g, pt: (0, 0)),
                  pl.BlockSpec((1, dv), lambda b, ph, g, pt: (0, 0))]
                 + [pl.BlockSpec((None, None, n_kh, PAGE_SIZE, HEAD_DIM), k_map(j)) for j in range(pp)]
                 + [pl.BlockSpec((None, None, hv, PAGE_SIZE, dv), v_map(j)) for j in range(pp)],
        out_specs=per_b(n_new, hv * dv),
        scratch_shapes=[pltpu.VMEM((n_keys, LANES), f32),
                        pltpu.VMEM((n_keys // MOBA_BLOCK, LANES), f32),
                        pltpu.VMEM((n_keys // MOBA_BLOCK, LANES), f32),
                        pltpu.VMEM((n_keys // MOBA_BLOCK, LANES), f32),
                        pltpu.VMEM((8, LANES), f32),
                        pltpu.VMEM((PAGE_SIZE, LANES), f32),
                        pltpu.VMEM((hv, LANES, dv), f32)],
    )
    return pl.pallas_call(
        functools.partial(_decode_attn_body, mode=mode, n_steps=n_steps, n_new=n_new, lam_init=lam_init),
        out_shape=jax.ShapeDtypeStruct((B, n_new, hv * dv), f32),
        grid_spec=grid_spec,
        compiler_params=_cparams("arbitrary", "arbitrary", "arbitrary"),
        name="decode_attn_" + mode,
    )(page_table, qx, knew, vnew, diff_lambda, norm_w.reshape(1, dv),
      *([cache_k] * pp), *([cache_v] * pp))


def _branch_body(oa_ref, ob_ref, oc_ref, wa_ref, wb_ref, wc_ref, ba_ref, bb_ref, bc_ref, o_ref):
    m = (_sigmoid(ba_ref[...]) * _dot(oa_ref[...], wa_ref[...])
         + _sigmoid(bb_ref[...]) * _dot(ob_ref[...], wb_ref[...])
         + _sigmoid(bc_ref[...]) * _dot(oc_ref[...], wc_ref[...]))
    o_ref[...] = m.astype(o_ref.dtype)


def branch_merge(oa, ob, oc, wa, wb, wc, z, tm):
    M = oa.shape[0]
    D = wa.shape[1]
    tn = 512
    assert Z_BR % tn == 0 and D % tn == 0 and M % tm == 0
    br0 = Z_BR // tn
    nd = D // tn
    act = lambda w: pl.BlockSpec((tm, w), lambda i, j: (i, 0))
    wsp = lambda kdim: pl.BlockSpec((kdim, tn), lambda i, j: (0, j))
    brs = lambda n: pl.BlockSpec((tm, tn), lambda i, j: (i, br0 + n * nd + j))
    return pl.pallas_call(
        _branch_body,
        out_shape=jax.ShapeDtypeStruct((M, D), bf16),
        grid=(M // tm, nd),
        in_specs=[act(oa.shape[1]), act(ob.shape[1]), act(oc.shape[1]),
                  wsp(wa.shape[0]), wsp(wb.shape[0]), wsp(wc.shape[0]),
                  brs(0), brs(1), brs(2)],
        out_specs=pl.BlockSpec((tm, tn), lambda i, j: (i, j)),
        compiler_params=_cparams("parallel", "parallel"),
        name="branch_merge",
    )(oa, ob, oc, wa, wb, wc, z, z, z)


def _proj_resid_body(a_ref, w_ref, x_ref, g_ref, o_ref):
    g = g_ref[0] if len(g_ref.shape) == 3 else g_ref[...]
    o_ref[...] = x_ref[...] + g * _dot(a_ref[...], w_ref[...])


def proj_residual(a, w, x, gate, rows_per_seq, tm):
    M, K = a.shape
    D = w.shape[1]
    tn = 512
    if rows_per_seq % tm == 0:
        per = rows_per_seq // tm
        g_arg = gate.reshape(gate.shape[0], 1, D)
        g_spec = pl.BlockSpec((1, 1, tn), lambda i, j: (i // per, 0, j))
    else:
        g_arg = jnp.repeat(gate, rows_per_seq, axis=0)
        g_spec = pl.BlockSpec((tm, tn), lambda i, j: (i, j))
    return pl.pallas_call(
        _proj_resid_body,
        out_shape=jax.ShapeDtypeStruct((M, D), f32),
        grid=(M // tm, D // tn),
        in_specs=[pl.BlockSpec((tm, K), lambda i, j: (i, 0)),
                  pl.BlockSpec((K, tn), lambda i, j: (0, j)),
                  pl.BlockSpec((tm, tn), lambda i, j: (i, j)),
                  g_spec],
        out_specs=pl.BlockSpec((tm, tn), lambda i, j: (i, j)),
        compiler_params=_cparams("parallel", "parallel"),
        name="proj_residual",
    )(a, w, x, g_arg)


def _router_body(h_ref, w_ref, b_ref, idx_ref, gate_ref):
    logits = _dot_nt(h_ref[...], w_ref[...].astype(bf16)) + b_ref[...]
    lane = lax.broadcasted_iota(jnp.int32, logits.shape, 1)
    logits = jnp.where(lane < N_EXPERTS, logits, -jnp.inf)
    idx_out = jnp.zeros(logits.shape, jnp.int32)
    val_out = jnp.full(logits.shape, -jnp.inf, f32)
    for r in range(TOP_K):
        m = jnp.max(logits, axis=-1, keepdims=True)
        idx = jnp.min(jnp.where(logits == m, lane, LANES), axis=-1, keepdims=True)
        idx_out = jnp.where(lane == r, idx, idx_out)
        val_out = jnp.where(lane == r, m, val_out)
        logits = jnp.where(lane == idx, -jnp.inf, logits)
    e = jnp.exp(val_out - jnp.max(val_out, axis=-1, keepdims=True))
    idx_ref[...] = idx_out
    gate_ref[...] = e / jnp.sum(e, axis=-1, keepdims=True)


def router(h, w_router, b_router, tm):
    T, D = h.shape
    w = jnp.pad(w_router.T, ((0, LANES - N_EXPERTS), (0, 0)))
    b = jnp.pad(b_router, (0, LANES - N_EXPERTS)).reshape(1, LANES)
    return pl.pallas_call(
        _router_body,
        out_shape=[jax.ShapeDtypeStruct((T, LANES), jnp.int32), jax.ShapeDtypeStruct((T, LANES), f32)],
        grid=(T // tm,),
        in_specs=[pl.BlockSpec((tm, D), lambda i: (i, 0)),
                  pl.BlockSpec((LANES, D), lambda i: (0, 0)),
                  pl.BlockSpec((1, LANES), lambda i: (0, 0))],
        out_specs=[pl.BlockSpec((tm, LANES), lambda i: (i, 0))] * 2,
        compiler_params=_cparams("parallel"),
        name="router",
    )(h, w, b)


def _cast_rows(src_ref, dst_ref):
    step = 512
    for r0 in range(0, src_ref.shape[0], step):
        dst_ref[r0:r0 + step, :] = src_ref[r0:r0 + step, :].astype(bf16)


def _cast_parts(part_refs, dst_ref):
    rows = part_refs[0].shape[0]
    for q, ref in enumerate(part_refs):
        _cast_rows(ref, dst_ref.at[q * rows:(q + 1) * rows, :])


def _moe_up_body(ie, ij, ib, ifirst, ivalid, x_ref, *rest):
    ns = MOE_W_SPLIT
    wg_refs, wu_refs = rest[:ns], rest[ns:2 * ns]
    bg_ref, bu_ref, o_ref, wg_s, wu_s = rest[2 * ns:]
    i = pl.program_id(0)

    @pl.when(ifirst[i] == 1)
    def _():
        _cast_parts(wg_refs, wg_s)
        _cast_parts(wu_refs, wu_s)

    def act(x):
        gate = jnp.minimum(_dot(x, wg_s[...]) + bg_ref[...], SWIGLU_LIMIT)
        up = jnp.clip(_dot(x, wu_s[...]) + bu_ref[...], -SWIGLU_LIMIT, SWIGLU_LIMIT)
        return ((up + 1.0) * gate * _sigmoid(SWIGLU_ALPHA * gate)).astype(o_ref.dtype)

    _moe_block_cases(ivalid[i], o_ref, lambda rows: act(x_ref[rows, :]))


def _moe_block_cases(kind, o_ref, compute):
    bm, tn = o_ref.shape
    for q in range(1, bm // MOE_GRAN + 1):
        used = q * MOE_GRAN

        @pl.when(kind == q)
        def _(used=used):
            o_ref[:used, :] = compute(slice(0, used))
            if used < bm:
                o_ref[used:, :] = jnp.zeros((bm - used, tn), o_ref.dtype)

    @pl.when(kind == 0)
    def _():
        o_ref[...] = jnp.zeros_like(o_ref)


def _moe_down_body(ie, ij, ib, ifirst, ivalid, a_ref, *rest):
    ns = MOE_W_SPLIT
    w_refs = rest[:ns]
    b_ref, o_ref, w_s = rest[ns:]
    i = pl.program_id(0)

    @pl.when(ifirst[i] == 1)
    def _():
        _cast_parts(w_refs, w_s)

    _moe_block_cases(ivalid[i], o_ref, lambda rows: _dot(a_ref[rows, :], w_s[...]) + b_ref[...])


def _moe_items(nb_e, blk_start, counts, bm, n_blocks_max, n_j):
    n_items = n_blocks_max * n_j
    total = jnp.sum(nb_e) * n_j
    item_end = jnp.cumsum(nb_e) * n_j
    i = jnp.arange(n_items, dtype=jnp.int32)
    i_c = jnp.minimum(i, total - 1)
    e = jnp.minimum(jnp.sum((i_c[:, None] >= item_end[None, :]).astype(jnp.int32), axis=1), N_EXPERTS - 1)
    r = i_c - blk_start[e] * n_j
    nb = jnp.maximum(nb_e[e], 1)
    valid = (i < total).astype(jnp.int32)
    j = (r // nb).astype(jnp.int32)
    b = jnp.where(valid == 1, blk_start[e] + r % nb, n_blocks_max).astype(jnp.int32)
    first = ((r % nb == 0) & (valid == 1)).astype(jnp.int32)
    rows_used = counts[e] - (r % nb) * bm
    quarters = jnp.clip((rows_used + MOE_GRAN - 1) // MOE_GRAN, 1, bm // MOE_GRAN)
    kind = jnp.where(valid == 1, quarters, 0).astype(jnp.int32)
    return e, j, b, first, kind


def moe_ffn(h, layer, w_router, b_router, w_gu, b_gu, w_dn, b_dn):
    T, D = h.shape
    de = w_dn.shape[2]
    bm = MOE_BM
    idx128, gate128 = router(h, w_router[layer], b_router[layer], tm=T // 12)
    top_idx, gates = idx128[:, :TOP_K], gate128[:, :TOP_K]
    A = T * TOP_K
    flat_e = top_idx.reshape(A)
    ch = LANES
    assert A % ch == 0
    onehot = (flat_e[:, None] == jnp.arange(N_EXPERTS, dtype=jnp.int32)[None, :]).astype(f32).reshape(A // ch, ch, N_EXPERTS)
    within = jnp.einsum("ij,cjk->cik", jnp.tril(jnp.ones((ch, ch), f32)), onehot)
    chunk_tot = within[:, -1, :]
    before = jnp.cumsum(chunk_tot, axis=0) - chunk_tot
    counts = jnp.sum(chunk_tot, axis=0).astype(jnp.int32)
    rank = (jnp.sum((within + before[:, None, :]) * onehot, axis=2) - 1.0).astype(jnp.int32).reshape(A)
    nb_e = (counts + bm - 1) // bm
    blk_start = jnp.cumsum(nb_e) - nb_e
    dest = (blk_start[flat_e] * bm + rank).astype(jnp.int32)
    n_blocks_max = (A + N_EXPERTS * (bm - 1)) // bm
    P = (n_blocks_max + 1) * bm
    slot_tok = (jnp.arange(P, dtype=jnp.int32) % T).at[dest].set(jnp.arange(A, dtype=jnp.int32) // TOP_K)
    inv = dest.reshape(T, TOP_K)
    xb = h[slot_tok]

    ns = MOE_W_SPLIT

    def w_part(kdim, tn, q, j0):
        return pl.BlockSpec((None, None, kdim // ns, tn), lambda i, e, j, b, f, v: (layer, e[i], q, j[i] + j0))

    tn1 = 512
    nj1 = de // tn1
    items =_moe_items(nb_e, blk_start, counts, bm, n_blocks_max, nj1)
    b_gu4 = b_gu.reshape(b_gu.shape[0], N_EXPERTS, 1, 2 * de)
    act = pl.pallas_call(
        _moe_up_body,
        out_shape=jax.ShapeDtypeStruct((P, de), bf16),
        grid_spec=pltpu.PrefetchScalarGridSpec(
            num_scalar_prefetch=5,
            grid=(n_blocks_max * nj1,),
            in_specs=[pl.BlockSpec((bm, D), lambda i, e, j, b, f, v: (b[i], 0))]
                     + [w_part(D, tn1, q, 0) for q in range(ns)] + [w_part(D, tn1, q, nj1) for q in range(ns)]
                     + [pl.BlockSpec((None, None, 1, tn1), lambda i, e, j, b, f, v: (layer, e[i], 0, j[i])),
                        pl.BlockSpec((None, None, 1, tn1), lambda i, e, j, b, f, v: (layer, e[i], 0, j[i] + nj1))],
            out_specs=pl.BlockSpec((bm, tn1), lambda i, e, j, b, f, v: (b[i], j[i])),
            scratch_shapes=[pltpu.VMEM((D, tn1), bf16), pltpu.VMEM((D, tn1), bf16)]),
        compiler_params=_cparams("arbitrary"),
        name="moe_up",
    )(*items, xb, *([w_gu] * (2 * ns)), b_gu4, b_gu4)

    tn2 = 1024
    nj2 = D // tn2
    items2 = _moe_items(nb_e, blk_start, counts, bm, n_blocks_max, nj2)
    b_dn4 = b_dn.reshape(b_dn.shape[0], N_EXPERTS, 1, D)
    yb = pl.pallas_call(
        _moe_down_body,
        out_shape=jax.ShapeDtypeStruct((P, D), f32),
        grid_spec=pltpu.PrefetchScalarGridSpec(
            num_scalar_prefetch=5,
            grid=(n_blocks_max * nj2,),
            in_specs=[pl.BlockSpec((bm, de), lambda i, e, j, b, f, v: (b[i], 0))]
                     + [w_part(de, tn2, q, 0) for q in range(ns)]
                     + [pl.BlockSpec((None, None, 1, tn2), lambda i, e, j, b, f, v: (layer, e[i], 0, j[i]))],
            out_specs=pl.BlockSpec((bm, tn2), lambda i, e, j, b, f, v: (b[i], j[i])),
            scratch_shapes=[pltpu.VMEM((de, tn2), bf16)]),
        compiler_params=_cparams("arbitrary"),
        name="moe_down",
    )(*items2, act, *([w_dn] * ns), b_dn4)
    return sum(gates[:, k:k + 1] * yb[inv[:, k]] for k in range(TOP_K))


def _mixer(l, x, mod, past, W):
    B, L, D = x.shape
    M = B * L
    prompt = past is None
    sh1, sc1, g1 = mod[0], mod[1], mod[2]
    h = rms_norm_mod(x, W["norm_mix"][l], sc1, sh1).reshape(M, D)
    tm = 1024 if M % 1024 == 0 else M
    z = matmul_nt(h, W["w_in_main"], layer=l, tm=tm, tn=512 if prompt else 1536, name="in_proj")
    zs = matmul_nt(h, W["w_in_small"][l], tm=tm, tn=2 * LANES, name="in_proj_small")
    nz = z.shape[1]
    z3 = z.reshape(B, L, nz)

    if prompt:
        zqkv, zs3, tile = z3, zs.reshape(B, L, 2 * LANES), GDN_TILE
        buf8 = jnp.zeros((B, 8, GDN_CONV_W), f32)
        state0 = jnp.zeros((B, GDN_HEADS, HEAD_DIM, HEAD_DIM), f32)
        gate_arr, gate_col0 = z3, Z_GATE_A
    else:
        tile = lp = GDN_CHUNK
        assert L <= lp
        rpad = ((0, 0), (0, lp - L), (0, 0))
        zqkv = jnp.pad(z3[:, :, :GDN_CONV_W], rpad)
        zs3 = jnp.pad(zs.reshape(B, L, 2 * LANES), rpad)
        buf8 = jnp.pad(past["state_conv"][l], ((0, 0), (8 - (GDN_CONV - 1), 0), (0, 0)))
        state0 = past["state_gdn"][l]
        gate_arr, gate_col0 = jnp.pad(z3[:, :, Z_GATE_A:Z_GATE_A + GDN_W], rpad), 0
    qa, ka, va, gcum, beta = gdn_prep(zqkv, zs3, buf8, W["conv_w"][l], W["a_log"][l], W["dt_bias"][l], L, tile)
    w1, w2, qk, qg, kd = gdn_intra(qa, ka, va, gcum, beta, tile, min(GDN_CHUNK, L))
    o_a, gdn_new = gdn_seq(w1, w2, qg, kd, qk, gcum, gate_arr, gate_col0, state0, W["gdn_norm_w"][l], tile)
    o_a = o_a[:, :L].reshape(M, GDN_W)
    conv_new = (jnp.concatenate([buf8[:, 8 - (GDN_CONV - 1):], z3[:, :, :GDN_CONV_W]], axis=1))[:, -(GDN_CONV - 1):]

    lam_init = 0.8 - 0.6 * math.exp(-0.3 * l)
    if prompt:
        o_b, kb, vb = moba_prompt(z3, B, L)
        o_c, kc, vc = diff_prompt(z3, B, L, W["diff_lambda"][l], W["diff_norm_w"][l], lam_init)
        kb, vb, kc, vc = (t.transpose(0, 2, 1, 3) for t in (kb, vb, kc, vc))
    else:
        kb = z3[:, :, Z_KB:Z_KB + MOBA_W].reshape(B, L, MOBA_HEADS, HEAD_DIM)
        vb = z3[:, :, Z_VB:Z_VB + MOBA_W].reshape(B, L, MOBA_HEADS, HEAD_DIM)
        kc = z3[:, :, Z_KC:Z_KC + DIFF_W].reshape(B, L, 2 * DIFF_HEADS, HEAD_DIM)
        vc = z3[:, :, Z_VC:Z_VC + DIFF_W].reshape(B, L, DIFF_HEADS, 2 * HEAD_DIM)
        qb = z3[:, :, Z_QB:Z_QB + MOBA_W].reshape(B, L, MOBA_HEADS, HEAD_DIM)
        qc = z3[:, :, Z_QC:Z_QC + DIFF_W].reshape(B, L, 2 * DIFF_HEADS, HEAD_DIM)
        pt = past["page_table"]
        o_b = decode_attention("moba", l, pt, qb, kb, vb, past["cache_moba_k"], past["cache_moba_v"],
                               W["diff_lambda"][l], jnp.ones((HEAD_DIM,), f32), lam_init)
        o_c = decode_attention("diff", l, pt, qc, kc, vc, past["cache_diff_k"], past["cache_diff_v"],
                               W["diff_lambda"][l], W["diff_norm_w"][l], lam_init)
    o_b = o_b.reshape(M, MOBA_W).astype(bf16)
    o_c = o_c.reshape(M, DIFF_W).astype(bf16)

    wa, wb, wc = W["w_branch_split"][l]
    merged = branch_merge(o_a, o_b, o_c, wa, wb, wc, z, tm)
    x_new = proj_residual(merged, W["w_out_bf"][l], x.reshape(M, D), g1, L, tm).reshape(B, L, D)
    return x_new, (gdn_new, conv_new, kb, vb, kc, vc)


def kernel(x_prompt, x_sample, cache_moba_k, cache_moba_v, cache_diff_k, cache_diff_v, state_gdn, state_conv,
           page_table, c_prompt, c_sample, w_mod, b_mod, norm_mix, norm_ffn, w_in, conv_w, a_log, dt_bias,
           gdn_norm_w, diff_lambda, diff_norm_w, w_branch, w_out, w_router, b_router, w_gu, b_gu, w_dn, b_dn,
           norm_out):
    depth = w_in.shape[0]
    D = x_prompt.shape[-1]
    Bp, Lp, _ = x_prompt.shape
    Bs, Ls, _ = x_sample.shape
    n_dec = 2 * GDN_HEADS
    small0 = GDN_CONV_W

    w_in_main = repack_rows_bf16(jnp.swapaxes(w_in, 1, 2), small0, n_dec, 512)
    w_in_small = [jnp.zeros((2 * LANES, D), f32)
                  .at[:GDN_HEADS].set(w_in[l, :, small0:small0 + GDN_HEADS].T)
                  .at[LANES:LANES + GDN_HEADS].set(w_in[l, :, small0 + GDN_HEADS:small0 + n_dec].T)
                  for l in range(depth)]
    w_branch_split = [(w_branch[l][:GDN_W].astype(bf16), w_branch[l][GDN_W:GDN_W + MOBA_W].astype(bf16),
                       w_branch[l][GDN_W + MOBA_W:].astype(bf16)) for l in range(depth)]
    W = dict(norm_mix=norm_mix, norm_ffn=norm_ffn, w_in_main=w_in_main, w_in_small=w_in_small, conv_w=conv_w,
             a_log=a_log, dt_bias=dt_bias, gdn_norm_w=gdn_norm_w, diff_lambda=diff_lambda, diff_norm_w=diff_norm_w,
             w_branch_split=w_branch_split, w_out_bf=[w_out[l].astype(bf16) for l in range(depth)])
    head_major = lambda c: c.transpose(0, 1, 3, 2, 4)
    past = dict(page_table=page_table, state_gdn=state_gdn, state_conv=state_conv,
                cache_moba_k=head_major(cache_moba_k), cache_moba_v=head_major(cache_moba_v),
                cache_diff_k=head_major(cache_diff_k), cache_diff_v=head_major(cache_diff_v))

    c_all = jnp.concatenate([c_prompt, c_sample], axis=0)
    n_c = c_all.shape[0]
    c_act = jnp.pad(c_all * jax.nn.sigmoid(c_all), ((0, -n_c % 16), (0, 0)))
    xp, xs = x_prompt, x_sample
    rows_p, rows_s = [], []
    Tp, Ts = Bp * Lp, Bs * Ls
    for l in range(depth):
        mod = matmul_bias(c_act, w_mod, l, b_mod[l], tn=1024, k_split=4, name="adaln_mod")
        mod = jnp.split(mod[:n_c], N_MOD, axis=-1)
        mod_p = [m[:Bp] for m in mod]
        mod_s = [m[Bp:] for m in mod]
        xp, rp = _mixer(l, xp, mod_p, None, W)
        xs, rs = _mixer(l, xs, mod_s, past, W)
        rows_p.append(rp)
        rows_s.append(rs)
        hp = rms_norm_mod(xp, norm_ffn[l], mod_p[4], mod_p[3]).reshape(Tp, D)
        hs = rms_norm_mod(xs, norm_ffn[l], mod_s[4], mod_s[3]).reshape(Ts, D)
        y = moe_ffn(jnp.concatenate([hp, hs], axis=0), l, w_router, b_router, w_gu, b_gu, w_dn, b_dn)
        xp = xp + mod_p[5][:, None, :] * y[:Tp].reshape(Bp, Lp, D)
        xs = xs + mod_s[5][:, None, :] * y[Tp:].reshape(Bs, Ls, D)
    yp = rms_norm_mod(xp, norm_out, out_dtype=f32)
    ys = rms_norm_mod(xs, norm_out, out_dtype=f32)
    stack = lambda rows, i: jnp.stack([r[i] for r in rows])
    return ((yp, ys) + tuple(stack(rows_p, i) for i in range(6)) + tuple(stack(rows_s, i) for i in range(6)))
```
